```python
import jax, jax.numpy as jnp
from jax import lax
import numpy as np

D_MODEL = 1024
BATCH = 8
SEQ = 8192
DEPTH = 1

HEAD_DIM = 64
NA_HEADS = 8
NB_HEADS = 8
NB_KV_HEADS = 2
GRID_W = 64
NA_KH_MAX = 8
NA_KW = 16
WIN = 128
BLOCK = 128
ROPE_THETA = 10000.0
D_FF = 2816
EPS = 1e-6

WIDTH_A = NA_HEADS * HEAD_DIM
WIDTH_BQ = NB_HEADS * HEAD_DIM
WIDTH_BKV = NB_KV_HEADS * HEAD_DIM
D_IN = 3 * WIDTH_A + WIDTH_BQ + 2 * WIDTH_BKV + 2 * D_MODEL
SPLITS = (WIDTH_A, 2 * WIDTH_A, 3 * WIDTH_A,
          3 * WIDTH_A + WIDTH_BQ,
          3 * WIDTH_A + WIDTH_BQ + WIDTH_BKV,
          3 * WIDTH_A + WIDTH_BQ + 2 * WIDTH_BKV,
          3 * WIDTH_A + WIDTH_BQ + 2 * WIDTH_BKV + D_MODEL)

kernel_name = "hybrid_natten_swa_gated_macaron"


def rms_norm(x, g):
    xf = x.astype(jnp.float32)
    y = xf * lax.rsqrt(jnp.mean(xf * xf, axis=-1, keepdims=True) + EPS)
    return (y * g.astype(jnp.float32)).astype(x.dtype)


def swiglu(x, w_gate, w_up, w_down):
    return (jax.nn.silu(x @ w_gate) * (x @ w_up)) @ w_down


def rope(x, pos):
    half = x.shape[-1] // 2
    inv = ROPE_THETA ** (-jnp.arange(half, dtype=jnp.float32) / half)
    ang = pos.astype(jnp.float32)[:, None] * inv[None, :]
    cos = jnp.cos(ang)[None, :, None, :]
    sin = jnp.sin(ang)[None, :, None, :]
    x1 = x[..., :half].astype(jnp.float32)
    x2 = x[..., half:].astype(jnp.float32)
    out = jnp.concatenate([x1 * cos - x2 * sin, x2 * cos + x1 * sin], axis=-1)
    return out.astype(x.dtype)


def neighbourhood_attention_2d(q, k, v, rpb):
    B, T, H, dh = q.shape
    rows = T // GRID_W
    kh = min(NA_KH_MAX, rows)
    q = q.reshape(B, rows, GRID_W, H, dh)
    k = k.reshape(B, rows, GRID_W, H, dh)
    v = v.reshape(B, rows, GRID_W, H, dh)
    cols = jnp.arange(GRID_W)
    col_start = jnp.clip(cols - NA_KW // 2, 0, GRID_W - NA_KW)
    col_idx = col_start[:, None] + jnp.arange(NA_KW)[None, :]
    col_off = col_idx - cols[:, None] + (NA_KW - 1)
    rpb = rpb.astype(jnp.float32)
    scale = dh ** -0.5

    def row_block(args):
        r, q_r = args
        rs = jnp.clip(r - kh // 2, 0, rows - kh)
        k_slab = lax.dynamic_slice_in_dim(k, rs, kh, axis=1)
        v_slab = lax.dynamic_slice_in_dim(v, rs, kh, axis=1)
        k_g = k_slab[:, :, col_idx]
        v_g = v_slab[:, :, col_idx]
        row_off = rs + jnp.arange(kh) - r + (NA_KH_MAX - 1)
        bias = rpb[:, row_off[None, :, None], col_off[:, None, :]]
        s = jnp.einsum('bchd,bicjhd->bhcij', q_r, k_g).astype(jnp.float32) * scale
        s = s + bias[None]
        p = jax.nn.softmax(s.reshape(B, H, GRID_W, kh * NA_KW), axis=-1)
        p = p.reshape(B, H, GRID_W, kh, NA_KW).astype(v.dtype)
        return jnp.einsum('bhcij,bicjhd->bchd', p, v_g)

    out = lax.map(row_block, (jnp.arange(rows), jnp.moveaxis(q, 1, 0)))
    return jnp.moveaxis(out, 0, 1).reshape(B, T, H * dh)


def windowed_gqa_sink(q, k, v, sink):
    B, T, Hq, dh = q.shape
    Hkv = k.shape[2]
    G = Hq // Hkv
    nb = T // BLOCK
    qb = q.reshape(B, nb, BLOCK, Hkv, G, dh)

    def band(a):
        ap = jnp.pad(a, ((0, 0), (BLOCK, BLOCK), (0, 0), (0, 0)))
        ab = ap.reshape(B, nb + 2, BLOCK, Hkv, dh)
        return jnp.concatenate([ab[:, :-2], ab[:, 1:-1], ab[:, 2:]], axis=2)

    kb = band(k)
    vb = band(v)
    qpos = jnp.arange(nb)[:, None] * BLOCK + jnp.arange(BLOCK)[None, :]
    kpos = jnp.arange(nb)[:, None] * BLOCK - BLOCK + jnp.arange(3 * BLOCK)[None, :]
    kp = kpos[:, None, :]
    mask = (jnp.abs(qpos[:, :, None] - kp) <= WIN) & (kp >= 0) & (kp < T)
    s = jnp.einsum('bnqhgd,bnkhd->bnhgqk', qb, kb).astype(jnp.float32) * (dh ** -0.5)
    s = jnp.where(mask[None, :, None, None], s, -jnp.inf)
    sink_l = sink.astype(jnp.float32).reshape(Hkv, G)[None, None, :, :, None, None]
    m = jnp.maximum(jnp.max(s, axis=-1, keepdims=True), sink_l)
    e = jnp.exp(s - m)
    p = e / (jnp.sum(e, axis=-1, keepdims=True) + jnp.exp(sink_l - m))
    o = jnp.einsum('bnhgqk,bnkhd->bnqhgd', p.astype(v.dtype), vb)
    return o.reshape(B, T, Hq * dh)


def setup_inputs(seed: int = 0) -> dict:
    key = jax.random.key(seed)
    ks = jax.random.split(key, 20)
    f32 = jnp.float32

    def w(k, shape, fan_in):
        return jax.random.normal(k, shape, f32) * (fan_in ** -0.5)

    def gain(k, shape):
        return 1.0 + 0.02 * jax.random.normal(k, shape, f32)

    return {
        "x": jax.random.normal(ks[0], (BATCH, SEQ, D_MODEL), f32),
        "ffn1_norm": gain(ks[1], (DEPTH, D_MODEL)),
        "ffn1_w_gate": w(ks[2], (DEPTH, D_MODEL, D_FF), D_MODEL),
        "ffn1_w_up": w(ks[3], (DEPTH, D_MODEL, D_FF), D_MODEL),
        "ffn1_w_down": w(ks[4], (DEPTH, D_FF, D_MODEL), D_FF),
        "mix_norm": gain(ks[5], (DEPTH, D_MODEL)),
        "w_in": w(ks[6], (DEPTH, D_MODEL, D_IN), D_MODEL),
        "na_rpb": 0.1 * jax.random.normal(ks[7], (DEPTH, NA_HEADS, 2 * NA_KH_MAX - 1, 2 * NA_KW - 1), f32),
        "sink_logit": 0.5 * jax.random.normal(ks[8], (DEPTH, NB_HEADS), f32),
        "w_branch_a": w(ks[9], (DEPTH, WIDTH_A, D_MODEL), WIDTH_A),
        "w_branch_b": w(ks[10], (DEPTH, WIDTH_BQ, D_MODEL), WIDTH_BQ),
        "w_out": w(ks[11], (DEPTH, D_MODEL, D_MODEL), D_MODEL),
        "ffn2_norm": gain(ks[12], (DEPTH, D_MODEL)),
        "ffn2_w_gate": w(ks[13], (DEPTH, D_MODEL, D_FF), D_MODEL),
        "ffn2_w_up": w(ks[14], (DEPTH, D_MODEL, D_FF), D_MODEL),
        "ffn2_w_down": w(ks[15], (DEPTH, D_FF, D_MODEL), D_FF),
        "final_norm": gain(ks[16], (D_MODEL,)),
    }


def reference(x, ffn1_norm, ffn1_w_gate, ffn1_w_up, ffn1_w_down, mix_norm, w_in, na_rpb,
              sink_logit, w_branch_a, w_branch_b, w_out, ffn2_norm, ffn2_w_gate, ffn2_w_up,
              ffn2_w_down, final_norm):
    B, T, _ = x.shape
    pos = jnp.arange(T)
    h = x
    for l in range(DEPTH):
        h = h + 0.5 * swiglu(rms_norm(h, ffn1_norm[l]), ffn1_w_gate[l], ffn1_w_up[l], ffn1_w_down[l])
        u = rms_norm(h, mix_norm[l])
        z = u @ w_in[l]
        qa, ka, va, qb, kb, vb, ga, gb = jnp.split(z, SPLITS, axis=-1)
        qa = qa.reshape(B, T, NA_HEADS, HEAD_DIM)
        ka = ka.reshape(B, T, NA_HEADS, HEAD_DIM)
        va = va.reshape(B, T, NA_HEADS, HEAD_DIM)
        ya = neighbourhood_attention_2d(qa, ka, va, na_rpb[l])
        qb = rope(qb.reshape(B, T, NB_HEADS, HEAD_DIM), pos)
        kb = rope(kb.reshape(B, T, NB_KV_HEADS, HEAD_DIM), pos)
        vb = vb.reshape(B, T, NB_KV_HEADS, HEAD_DIM)
        yb = windowed_gqa_sink(qb, kb, vb, sink_logit[l])
        merged = jax.nn.sigmoid(ga) * (ya @ w_branch_a[l]) + jax.nn.sigmoid(gb) * (yb @ w_branch_b[l])
        h = h + merged @ w_out[l]
        h = h + 0.5 * swiglu(rms_norm(h, ffn2_norm[l]), ffn2_w_gate[l], ffn2_w_up[l], ffn2_w_down[l])
    return rms_norm(h, final_norm)
```

```python
import functools

import jax
import jax.numpy as jnp
import numpy as np
from jax import lax
from jax.experimental import pallas as pl
from jax.experimental.pallas import tpu as pltpu

D_MODEL = 1024
HEAD_DIM = 64
NA_HEADS = 8
NB_HEADS = 8
NB_KV_HEADS = 2
GRID_W = 64
NA_KH = 8
NA_KW = 16
WIN = 128
ROPE_THETA = 10000.0
D_FF = 2816
EPS = 1e-6

WIDTH_A = NA_HEADS * HEAD_DIM
WIDTH_BQ = NB_HEADS * HEAD_DIM
WIDTH_BKV = NB_KV_HEADS * HEAD_DIM
QK_SCALE = HEAD_DIM ** -0.5
MASK_VALUE = -1e30

TOKEN_TILE = 512
FF_CHUNKS = ((0, 1536), (1536, 2816))
NA_Q_ROWS = 4
NA_K_ROWS = NA_Q_ROWS + NA_KH
SWA_Q = 256
SWA_K = SWA_Q + 2 * WIN
VMEM_LIMIT = 56 * 1024 * 1024

_BF16 = jnp.bfloat16
_F32 = jnp.float32


def _rms(x, g):
    return x * lax.rsqrt(jnp.mean(x * x, axis=-1, keepdims=True) + EPS) * g


def _sigmoid(x):
    return 1.0 / (1.0 + jnp.exp(-x))


def _swiglu(xb, wg_ref, wu_ref, wd_ref):
    acc = None
    for lo, hi in FF_CHUNKS:
        g = jnp.dot(xb, wg_ref[:, lo:hi], preferred_element_type=_F32)
        u = jnp.dot(xb, wu_ref[:, lo:hi], preferred_element_type=_F32)
        a = (g * _sigmoid(g) * u).astype(_BF16)
        part = jnp.dot(a, wd_ref[lo:hi, :], preferred_element_type=_F32)
        acc = part if acc is None else acc + part
    return acc


def _const_spec(shape):
    return pl.BlockSpec(shape, lambda *_: (0,) * len(shape), pipeline_mode=pl.Buffered(1))


def _ffn1_kernel(x_ref, g_ref, wg_ref, wu_ref, wd_ref, h_ref):
    x = x_ref[...]
    xb = _rms(x, g_ref[...]).astype(_BF16)
    h_ref[...] = x + 0.5 * _swiglu(xb, wg_ref, wu_ref, wd_ref)


def _ffn1(x2d, gain, wg, wu, wd):
    n, d = x2d.shape
    tile = pl.BlockSpec((TOKEN_TILE, d), lambda i: (i, 0))
    return pl.pallas_call(
        _ffn1_kernel,
        grid=(n // TOKEN_TILE,),
        in_specs=[tile, _const_spec((1, d)), _const_spec(wg.shape), _const_spec(wu.shape), _const_spec(wd.shape)],
        out_specs=tile,
        out_shape=jax.ShapeDtypeStruct((n, d), _F32),
        compiler_params=pltpu.CompilerParams(dimension_semantics=("parallel",), vmem_limit_bytes=VMEM_LIMIT),
        name="ffn1",
    )(x2d, gain, wg, wu, wd)


_OFF_QA, _OFF_KA, _OFF_VA = 0, WIDTH_A, 2 * WIDTH_A
_OFF_QB = 3 * WIDTH_A
_OFF_KVB = _OFF_QB + WIDTH_BQ
_OFF_GA = _OFF_KVB + 2 * WIDTH_BKV
_OFF_GB = _OFF_GA + D_MODEL
D_IN = _OFF_GB + D_MODEL


def _rope(x, cos, sin_signed, first_half):
    w = x.shape[-1]
    half = HEAD_DIM // 2
    partner = jnp.where(first_half, pltpu.roll(x, w - half, 1), pltpu.roll(x, half, 1))
    return x * cos + partner * sin_signed


def _inproj_kernel(h_ref, g_ref, w_ref, cos_ref, sin_ref,
                   qa_ref, ka_ref, va_ref, qb_ref, kb_ref, vb_ref, ga_ref, gb_ref):
    u = _rms(h_ref[...], g_ref[...]).astype(_BF16)

    def proj(lo, hi):
        return jnp.dot(u, w_ref[:, lo:hi], preferred_element_type=_F32)

    qa_ref[...] = (proj(_OFF_QA, _OFF_KA) * QK_SCALE).astype(_BF16)
    ka_ref[...] = proj(_OFF_KA, _OFF_VA).astype(_BF16)
    va_ref[...] = proj(_OFF_VA, _OFF_QB).astype(_BF16)

    cos = cos_ref[...]
    sin = sin_ref[...]
    lane = lax.broadcasted_iota(jnp.int32, cos.shape, 1)
    first = (lane % HEAD_DIM) < (HEAD_DIM // 2)
    reps = WIDTH_BQ // cos.shape[-1]
    cos_q = jnp.concatenate([cos] * reps, axis=-1)
    sin_q = jnp.concatenate([sin] * reps, axis=-1)
    first_q = jnp.concatenate([first] * reps, axis=-1)
    qb = _rope(proj(_OFF_QB, _OFF_KVB), cos_q, sin_q, first_q)
    qb_ref[...] = (qb * QK_SCALE).astype(_BF16)

    kv = proj(_OFF_KVB, _OFF_GA)
    kb_ref[...] = _rope(kv[:, :WIDTH_BKV], cos, sin, first).astype(_BF16)
    vb_ref[...] = kv[:, WIDTH_BKV:].astype(_BF16)

    ga_ref[...] = _sigmoid(proj(_OFF_GA, _OFF_GB)).astype(_BF16)
    gb_ref[...] = _sigmoid(proj(_OFF_GB, D_IN)).astype(_BF16)


def _inproj(h2d, gain, w_in, cos_t, sin_t, seq):
    n, d = h2d.shape
    tiles_per_seq = seq // TOKEN_TILE

    def tile(width):
        return pl.BlockSpec((TOKEN_TILE, width), lambda i: (i, 0))

    table = pl.BlockSpec((TOKEN_TILE, cos_t.shape[-1]), lambda i: (i % tiles_per_seq, 0))
    widths = (WIDTH_A, WIDTH_A, WIDTH_A, WIDTH_BQ, WIDTH_BKV, WIDTH_BKV, D_MODEL, D_MODEL)
    return pl.pallas_call(
        _inproj_kernel,
        grid=(n // TOKEN_TILE,),
        in_specs=[tile(d), _const_spec((1, d)), _const_spec(w_in.shape), table, table],
        out_specs=[tile(w) for w in widths],
        out_shape=[jax.ShapeDtypeStruct((n, w), _BF16) for w in widths],
        compiler_params=pltpu.CompilerParams(dimension_semantics=("parallel",), vmem_limit_bytes=VMEM_LIMIT),
        name="inproj",
    )(h2d, gain, w_in, cos_t, sin_t)


def _rope_tables(seq):
    half = HEAD_DIM // 2
    inv = ROPE_THETA ** (-jnp.arange(half, dtype=_F32) / half)
    ang = jnp.arange(seq).astype(_F32)[:, None] * inv[None, :]
    cos, sin = jnp.cos(ang), jnp.sin(ang)
    cos_h = jnp.concatenate([cos, cos], axis=-1)
    sin_h = jnp.concatenate([-sin, sin], axis=-1)
    return jnp.tile(cos_h, (1, 2)), jnp.tile(sin_h, (1, 2))


def _na_bias_tables(rpb, rows):
    steps = rows // NA_Q_ROWS
    tables = []
    for step in (0, 1, steps - 1):
        key_row0 = int(np.clip(step * NA_Q_ROWS - NA_KH // 2, 0, rows - NA_K_ROWS))
        qr = step * NA_Q_ROWS + np.arange(NA_Q_ROWS)
        kr = key_row0 + np.arange(NA_K_ROWS)
        rs = np.clip(qr - NA_KH // 2, 0, rows - NA_KH)
        row_ok = (kr[None, :] >= rs[:, None]) & (kr[None, :] < rs[:, None] + NA_KH)
        row_off = np.clip(kr[None, :] - qr[:, None] + NA_KH - 1, 0, 2 * NA_KH - 2)
        c = np.arange(GRID_W)
        cs = np.clip(c - NA_KW // 2, 0, GRID_W - NA_KW)
        col_ok = (c[None, :] >= cs[:, None]) & (c[None, :] < cs[:, None] + NA_KW)
        col_off = np.clip(c[None, :] - c[:, None] + NA_KW - 1, 0, 2 * NA_KW - 2)
        ro = row_off[:, None, :, None]
        co = col_off[None, :, None, :]
        ok = row_ok[:, None, :, None] & col_ok[None, :, None, :]
        vals = rpb[:, ro, co]
        vals = jnp.where(ok[None], vals, MASK_VALUE)
        tables.append(vals.reshape(rpb.shape[0], NA_Q_ROWS * GRID_W, NA_K_ROWS * GRID_W))
    return jnp.stack(tables)


def _natten_kernel(q_ref, k_ref, v_ref, bias_ref, o_ref, *, steps):
    a = pl.program_id(1)
    blk = NA_Q_ROWS * GRID_W
    nk = NA_K_ROWS * GRID_W
    start = pl.multiple_of(jnp.clip(a - 1, 0, steps - 3) * blk, blk)
    for h in range(NA_HEADS):
        cols = slice(h * HEAD_DIM, (h + 1) * HEAD_DIM)
        q = q_ref[:, cols]
        k = k_ref[pl.ds(start, nk), cols]
        v = v_ref[pl.ds(start, nk), cols]
        s = lax.dot_general(q, k, (((1,), (1,)), ((), ())), preferred_element_type=_F32)
        s = s + bias_ref[h]
        m = jnp.max(s, axis=-1, keepdims=True)
        p = jnp.exp(s - m)
        l = jnp.sum(p, axis=-1, keepdims=True)
        o = jnp.dot(p.astype(_BF16), v, preferred_element_type=_F32)
        o_ref[:, cols] = (o * (1.0 / l)).astype(_BF16)


def _natten(q, k, v, bias):
    b, t, w = q.shape
    blk = NA_Q_ROWS * GRID_W
    steps = t // blk
    assert NA_K_ROWS * GRID_W == 3 * blk and steps >= 3

    def cfg(bi, a):
        return (jnp.where(a == 0, 0, jnp.where(a == steps - 1, 2, 1)), 0, 0, 0)

    qspec = pl.BlockSpec((None, blk, w), lambda bi, a: (bi, a, 0))
    kvspec = pl.BlockSpec((None, t, w), lambda bi, a: (bi, 0, 0))
    bspec = pl.BlockSpec((None,) + bias.shape[1:], cfg)
    return pl.pallas_call(
        functools.partial(_natten_kernel, steps=steps),
        grid=(b, steps),
        in_specs=[qspec, kvspec, kvspec, bspec],
        out_specs=qspec,
        out_shape=jax.ShapeDtypeStruct((b, t, w), _BF16),
        compiler_params=pltpu.CompilerParams(
            dimension_semantics=("parallel", "arbitrary"), vmem_limit_bytes=VMEM_LIMIT),
        name="natten",
    )(q, k, v, bias)


def _swa_kernel(sink_ref, q_ref, k_ref, v_ref, o_ref, *, seq):
    n = pl.program_id(1)
    start = pl.multiple_of(jnp.clip(n * SWA_Q - WIN, 0, seq - SWA_K), WIN)
    qpos = n * SWA_Q + lax.broadcasted_iota(jnp.int32, (SWA_Q, SWA_K), 0)
    kpos = start + lax.broadcasted_iota(jnp.int32, (SWA_Q, SWA_K), 1)
    mask_bias = jnp.where(jnp.abs(qpos - kpos) <= WIN, 0.0, MASK_VALUE).astype(_F32)
    group = NB_HEADS // NB_KV_HEADS
    for j in range(NB_KV_HEADS):
        kv_cols = slice(j * HEAD_DIM, (j + 1) * HEAD_DIM)
        k = k_ref[pl.ds(start, SWA_K), kv_cols]
        v = v_ref[pl.ds(start, SWA_K), kv_cols]
        for g in range(group):
            h = j * group + g
            cols = slice(h * HEAD_DIM, (h + 1) * HEAD_DIM)
            sink = sink_ref[h]
            s = lax.dot_general(q_ref[:, cols], k, (((1,), (1,)), ((), ())), preferred_element_type=_F32)
            s = s + mask_bias
            m = jnp.maximum(jnp.max(s, axis=-1, keepdims=True), sink)
            p = jnp.exp(s - m)
            l = jnp.sum(p, axis=-1, keepdims=True) + jnp.exp(sink - m)
            o = jnp.dot(p.astype(_BF16), v, preferred_element_type=_F32)
            o_ref[:, cols] = (o * (1.0 / l)).astype(_BF16)


def _swa(q, k, v, sink):
    b, t, w = q.shape
    qspec = pl.BlockSpec((None, SWA_Q, w), lambda bi, n: (bi, n, 0))
    kvspec = pl.BlockSpec((None, t, k.shape[-1]), lambda bi, n: (bi, 0, 0))
    return pl.pallas_call(
        functools.partial(_swa_kernel, seq=t),
        grid=(b, t // SWA_Q),
        in_specs=[pl.BlockSpec(memory_space=pltpu.SMEM), qspec, kvspec, kvspec],
        out_specs=qspec,
        out_shape=jax.ShapeDtypeStruct((b, t, w), _BF16),
        compiler_params=pltpu.CompilerParams(
            dimension_semantics=("parallel", "arbitrary"), vmem_limit_bytes=VMEM_LIMIT),
        name="swa",
    )(sink, q, k, v)


def _merge_kernel(h_ref, ya_ref, yb_ref, ga_ref, gb_ref, wa_ref, wb_ref, wo_ref,
                  g2_ref, wg_ref, wu_ref, wd_ref, gf_ref, y_ref):
    ma = jnp.dot(ya_ref[...], wa_ref[...], preferred_element_type=_F32)
    mb = jnp.dot(yb_ref[...], wb_ref[...], preferred_element_type=_F32)
    merged = ga_ref[...].astype(_F32) * ma + gb_ref[...].astype(_F32) * mb
    h = h_ref[...] + jnp.dot(merged.astype(_BF16), wo_ref[...], preferred_element_type=_F32)
    hb = _rms(h, g2_ref[...]).astype(_BF16)
    h = h + 0.5 * _swiglu(hb, wg_ref, wu_ref, wd_ref)
    y_ref[...] = _rms(h, gf_ref[...])


def _merge(h2d, ya, yb, ga, gb, wa, wb, wo, g2, wg, wu, wd, gf):
    n, d = h2d.shape

    def tile(width):
        return pl.BlockSpec((TOKEN_TILE, width), lambda i: (i, 0))

    consts = (wa, wb, wo, g2, wg, wu, wd, gf)
    return pl.pallas_call(
        _merge_kernel,
        grid=(n // TOKEN_TILE,),
        in_specs=[tile(d), tile(ya.shape[-1]), tile(yb.shape[-1]), tile(d), tile(d)]
                 + [_const_spec(c.shape) for c in consts],
        out_specs=tile(d),
        out_shape=jax.ShapeDtypeStruct((n, d), _F32),
        compiler_params=pltpu.CompilerParams(dimension_semantics=("parallel",), vmem_limit_bytes=VMEM_LIMIT),
        name="merge_ffn2",
    )(h2d, ya, yb, ga, gb, *consts)


def kernel(x, ffn1_norm, ffn1_w_gate, ffn1_w_up, ffn1_w_down, mix_norm, w_in, na_rpb, sink_logit, w_branch_a, w_branch_b, w_out, ffn2_norm, ffn2_w_gate, ffn2_w_up, ffn2_w_down, final_norm):
    b, t, d = x.shape
    assert ffn1_norm.shape[0] == 1, "single-layer trunk: the final norm is fused into the layer's last kernel"
    rows = t // GRID_W
    cos_t, sin_t = _rope_tables(t)
    bf = lambda w: w[0].astype(_BF16)
    row = lambda g: g.reshape(1, d).astype(_F32)
    seq3 = lambda a: a.reshape(b, t, a.shape[-1])

    h = _ffn1(x.reshape(b * t, d), row(ffn1_norm), bf(ffn1_w_gate), bf(ffn1_w_up), bf(ffn1_w_down))
    qa, ka, va, qb, kb, vb, ga, gb = _inproj(h, row(mix_norm), bf(w_in), cos_t, sin_t, t)
    ya = _natten(seq3(qa), seq3(ka), seq3(va), _na_bias_tables(na_rpb[0].astype(_F32), rows))
    yb = _swa(seq3(qb), seq3(kb), seq3(vb), sink_logit[0].astype(_F32))
    y = _merge(h, ya.reshape(b * t, -1), yb.reshape(b * t, -1), ga, gb,
               bf(w_branch_a), bf(w_branch_b), bf(w_out),
               row(ffn2_norm), bf(ffn2_w_gate), bf(ffn2_w_up), bf(ffn2_w_down), row(final_norm))
    return y.reshape(b, t, d)
```

```python
import functools

import jax
import jax.numpy as jnp
import numpy as np
from jax import lax
from jax.experimental import pallas as pl
from jax.experimental.pallas import tpu as pltpu

D_MODEL = 1024
HEAD_DIM = 64
NA_HEADS = 8
NB_HEADS = 8
NB_KV_HEADS = 2
GRID_W = 64
NA_KH = 8
NA_KW = 16
WIN = 128
ROPE_THETA = 10000.0
D_FF = 2816
EPS = 1e-6

WIDTH_A = NA_HEADS * HEAD_DIM
WIDTH_BQ = NB_HEADS * HEAD_DIM
WIDTH_BKV = NB_KV_HEADS * HEAD_DIM
QK_SCALE = HEAD_DIM ** -0.5
MASK_VALUE = -1e30

TOKEN_TILE = 512
FF_CHUNKS = ((0, 1536), (1536, 2816))
NA_Q_ROWS = 4
NA_K_ROWS = NA_Q_ROWS + NA_KH
SWA_Q = 256
SWA_K = SWA_Q + 2 * WIN
VMEM_LIMIT = 56 * 1024 * 1024

_BF16 = jnp.bfloat16
_F32 = jnp.float32


def _rms(x, g):
    return x * lax.rsqrt(jnp.mean(x * x, axis=-1, keepdims=True) + EPS) * g


def _sigmoid(x):
    return 1.0 / (1.0 + jnp.exp(-x))


def _swiglu(xb, wg_ref, wu_ref, wd_ref):
    acc = None
    for lo, hi in FF_CHUNKS:
        g = jnp.dot(xb, wg_ref[:, lo:hi], preferred_element_type=_F32)
        u = jnp.dot(xb, wu_ref[:, lo:hi], preferred_element_type=_F32)
        a = (g * _sigmoid(g) * u).astype(_BF16)
        part = jnp.dot(a, wd_ref[lo:hi, :], preferred_element_type=_F32)
        acc = part if acc is None else acc + part
    return acc


def _const_spec(shape):
    return pl.BlockSpec(shape, lambda *_: (0,) * len(shape), pipeline_mode=pl.Buffered(1))


def _ffn1_kernel(x_ref, g_ref, wg_ref, wu_ref, wd_ref, h_ref):
    x = x_ref[...]
    xb = _rms(x, g_ref[...]).astype(_BF16)
    h_ref[...] = x + 0.5 * _swiglu(xb, wg_ref, wu_ref, wd_ref)


def _ffn1(x2d, gain, wg, wu, wd):
    n, d = x2d.shape
    tile = pl.BlockSpec((TOKEN_TILE, d), lambda i: (i, 0))
    return pl.pallas_call(
        _ffn1_kernel,
        grid=(n // TOKEN_TILE,),
        in_specs=[tile, _const_spec((1, d)), _const_spec(wg.shape), _const_spec(wu.shape), _const_spec(wd.shape)],
        out_specs=tile,
        out_shape=jax.ShapeDtypeStruct((n, d), _F32),
        compiler_params=pltpu.CompilerParams(dimension_semantics=("parallel",), vmem_limit_bytes=VMEM_LIMIT),
        name="ffn1",
    )(x2d, gain, wg, wu, wd)


_OFF_QA, _OFF_KA, _OFF_VA = 0, WIDTH_A, 2 * WIDTH_A
_OFF_QB = 3 * WIDTH_A
_OFF_KVB = _OFF_QB + WIDTH_BQ
_OFF_GA = _OFF_KVB + 2 * WIDTH_BKV
_OFF_GB = _OFF_GA + D_MODEL
D_IN = _OFF_GB + D_MODEL


def _rope(x, cos, sin_signed, first_half):
    w = x.shape[-1]
    half = HEAD_DIM // 2
    partner = jnp.where(first_half, pltpu.roll(x, w - half, 1), pltpu.roll(x, half, 1))
    return x * cos + partner * sin_signed


def _inproj_kernel(h_ref, g_ref, w_ref, cos_ref, sin_ref,
                   qat_ref, ka_ref, vat_ref, qbt_ref, kb_ref, vbt_ref, ga_ref, gb_ref):
    u = _rms(h_ref[...], g_ref[...]).astype(_BF16)

    def proj(lo, hi):
        return jnp.dot(u, w_ref[:, lo:hi], preferred_element_type=_F32)

    qat_ref[...] = (proj(_OFF_QA, _OFF_KA) * QK_SCALE).T.astype(_BF16)
    ka_ref[...] = proj(_OFF_KA, _OFF_VA).astype(_BF16)
    vat_ref[...] = proj(_OFF_VA, _OFF_QB).T.astype(_BF16)

    cos = cos_ref[...]
    sin = sin_ref[...]
    lane = lax.broadcasted_iota(jnp.int32, cos.shape, 1)
    first = (lane % HEAD_DIM) < (HEAD_DIM // 2)
    reps = WIDTH_BQ // cos.shape[-1]
    cos_q = jnp.concatenate([cos] * reps, axis=-1)
    sin_q = jnp.concatenate([sin] * reps, axis=-1)
    first_q = jnp.concatenate([first] * reps, axis=-1)
    qb = _rope(proj(_OFF_QB, _OFF_KVB), cos_q, sin_q, first_q)
    qbt_ref[...] = (qb * QK_SCALE).T.astype(_BF16)

    kv = proj(_OFF_KVB, _OFF_GA)
    kb_ref[...] = _rope(kv[:, :WIDTH_BKV], cos, sin, first).astype(_BF16)
    vbt_ref[...] = kv[:, WIDTH_BKV:].T.astype(_BF16)

    ga_ref[...] = _sigmoid(proj(_OFF_GA, _OFF_GB)).astype(_BF16)
    gb_ref[...] = _sigmoid(proj(_OFF_GB, D_IN)).astype(_BF16)


def _inproj(h2d, gain, w_in, cos_t, sin_t, seq):
    n, d = h2d.shape
    tiles_per_seq = seq // TOKEN_TILE

    def tile(width):
        return pl.BlockSpec((TOKEN_TILE, width), lambda i: (i, 0))

    table = pl.BlockSpec((TOKEN_TILE, cos_t.shape[-1]), lambda i: (i % tiles_per_seq, 0))
    def tposed(width):
        return pl.BlockSpec((None, width, TOKEN_TILE), lambda i: (i // tiles_per_seq, 0, i % tiles_per_seq))

    outs = [
        (True, WIDTH_A), (False, WIDTH_A), (True, WIDTH_A), (True, WIDTH_BQ), (False, WIDTH_BKV), (True, WIDTH_BKV),
        (False, D_MODEL), (False, D_MODEL)]
    return pl.pallas_call(
        _inproj_kernel,
        grid=(n // TOKEN_TILE,),
        in_specs=[tile(d), _const_spec((1, d)), _const_spec(w_in.shape), table, table],
        out_specs=[tposed(w) if t else tile(w) for t, w in outs],
        out_shape=[jax.ShapeDtypeStruct((n // seq, w, seq) if t else (n, w), _BF16) for t, w in outs],
        compiler_params=pltpu.CompilerParams(dimension_semantics=("parallel",), vmem_limit_bytes=VMEM_LIMIT),
        name="inproj",
    )(h2d, gain, w_in, cos_t, sin_t)


def _rope_tables(seq):
    half = HEAD_DIM // 2
    inv = ROPE_THETA ** (-jnp.arange(half, dtype=_F32) / half)
    ang = jnp.arange(seq).astype(_F32)[:, None] * inv[None, :]
    cos, sin = jnp.cos(ang), jnp.sin(ang)
    cos_h = jnp.concatenate([cos, cos], axis=-1)
    sin_h = jnp.concatenate([-sin, sin], axis=-1)
    return jnp.tile(cos_h, (1, 2)), jnp.tile(sin_h, (1, 2))


def _na_bias_tables(rpb, rows):
    heads, n_dr, n_dc = rpb.shape
    span = 2 * GRID_W
    ring = jnp.concatenate(
        [rpb[..., NA_KW - 1:], jnp.zeros((heads, n_dr, span - n_dc), rpb.dtype), rpb[..., :NA_KW - 1]], axis=-1)
    skew = jnp.tile(ring, (1, 1, GRID_W))[..., :GRID_W * (span - 1)].reshape(heads, n_dr, GRID_W, span - 1)
    c = np.arange(GRID_W)
    cs = np.clip(c - NA_KW // 2, 0, GRID_W - NA_KW)
    col_ok = (c[None, :] >= cs[:, None]) & (c[None, :] < cs[:, None] + NA_KW)
    toe = jnp.where(col_ok, skew[..., :GRID_W], MASK_VALUE)
    toe = jnp.swapaxes(toe, -1, -2)
    masked = jnp.full((heads, GRID_W, GRID_W), MASK_VALUE, rpb.dtype)

    steps = rows // NA_Q_ROWS
    tables = []
    for step in (0, 1, steps - 1):
        key_row0 = int(np.clip(step * NA_Q_ROWS - NA_KH // 2, 0, rows - NA_K_ROWS))
        key_blocks = []
        for kr in key_row0 + np.arange(NA_K_ROWS):
            query_blocks = []
            for qr in step * NA_Q_ROWS + np.arange(NA_Q_ROWS):
                rs = int(np.clip(qr - NA_KH // 2, 0, rows - NA_KH))
                inside = rs <= kr < rs + NA_KH
                query_blocks.append(toe[:, kr - qr + NA_KH - 1] if inside else masked)
            key_blocks.append(jnp.concatenate(query_blocks, axis=-1))
        tables.append(jnp.concatenate(key_blocks, axis=1))
    return jnp.stack(tables)


def _natten_kernel(qt_ref, k_ref, vt_ref, bias_ref, o_ref, *, steps):
    a = pl.program_id(1)
    blk = NA_Q_ROWS * GRID_W
    nk = NA_K_ROWS * GRID_W
    start = pl.multiple_of(jnp.clip(a - 1, 0, steps - 3) * blk, blk)
    pair = 2 * HEAD_DIM
    upper = lax.broadcasted_iota(jnp.int32, (pair, blk), 0) >= HEAD_DIM

    def scores(h):
        rows = slice(h // 2 * pair, (h // 2 + 1) * pair)
        k2 = k_ref[pl.ds(start, nk), rows]
        q2 = qt_ref[rows, :]
        qh = jnp.where(upper if h % 2 else ~upper, q2, jnp.zeros_like(q2))
        return jnp.dot(k2, qh, preferred_element_type=_F32) + bias_ref[h]

    def attend(h, s):
        m = jnp.max(s, axis=0, keepdims=True)
        e = jnp.exp(s - m)
        l = jnp.sum(e, axis=0, keepdims=True)
        vt = vt_ref[h * HEAD_DIM:(h + 1) * HEAD_DIM, pl.ds(start, nk)]
        return jnp.dot(vt, e.astype(_BF16), preferred_element_type=_F32) * (1.0 / l)

    outs = []
    s_next = scores(0)
    for h in range(NA_HEADS):
        s = s_next
        if h + 1 < NA_HEADS:
            s_next = scores(h + 1)
        outs.append(attend(h, s))
        if h % 2:
            rows = slice(h // 2 * pair, (h // 2 + 1) * pair)
            o_ref[:, rows] = jnp.concatenate(outs[-2:], axis=0).T.astype(_BF16)


def _natten(qt, k, vt, bias):
    b, t, w = k.shape
    blk = NA_Q_ROWS * GRID_W
    steps = t // blk
    assert NA_K_ROWS * GRID_W == 3 * blk and steps >= 3

    def cfg(bi, a):
        return (jnp.where(a == 0, 0, jnp.where(a == steps - 1, 2, 1)), 0, 0, 0)

    qtspec = pl.BlockSpec((None, w, blk), lambda bi, a: (bi, 0, a))
    kspec = pl.BlockSpec((None, t, w), lambda bi, a: (bi, 0, 0))
    vtspec = pl.BlockSpec((None, w, t), lambda bi, a: (bi, 0, 0))
    bspec = pl.BlockSpec((None,) + bias.shape[1:], cfg)
    return pl.pallas_call(
        functools.partial(_natten_kernel, steps=steps),
        grid=(b, steps),
        in_specs=[qtspec, kspec, vtspec, bspec],
        out_specs=pl.BlockSpec((None, blk, w), lambda bi, a: (bi, a, 0)),
        out_shape=jax.ShapeDtypeStruct((b, t, w), _BF16),
        compiler_params=pltpu.CompilerParams(
            dimension_semantics=("parallel", "arbitrary"), vmem_limit_bytes=VMEM_LIMIT),
        name="natten",
    )(qt, k, vt, bias)


def _swa_kernel(sink_ref, qt_ref, k_ref, vt_ref, o_ref, *, seq):
    n = pl.program_id(1)
    start = pl.multiple_of(jnp.clip(n * SWA_Q - WIN, 0, seq - SWA_K), WIN)
    kpos = start + lax.broadcasted_iota(jnp.int32, (SWA_K, SWA_Q), 0)
    qpos = n * SWA_Q + lax.broadcasted_iota(jnp.int32, (SWA_K, SWA_Q), 1)
    mask_bias = jnp.where(jnp.abs(qpos - kpos) <= WIN, 0.0, MASK_VALUE).astype(_F32)
    group = NB_HEADS // NB_KV_HEADS
    k2 = k_ref[pl.ds(start, SWA_K), :]
    zeros = jnp.zeros((HEAD_DIM, SWA_Q), _BF16)

    def scores(h):
        qh = qt_ref[h * HEAD_DIM:(h + 1) * HEAD_DIM, :]
        qh = jnp.concatenate([qh, zeros] if h // group == 0 else [zeros, qh], axis=0)
        return jnp.dot(k2, qh, preferred_element_type=_F32) + mask_bias

    def attend(h, s):
        sink = sink_ref[h]
        m = jnp.maximum(jnp.max(s, axis=0, keepdims=True), sink)
        e = jnp.exp(s - m)
        l = jnp.sum(e, axis=0, keepdims=True) + jnp.exp(sink - m)
        j = h // group
        vt = vt_ref[j * HEAD_DIM:(j + 1) * HEAD_DIM, pl.ds(start, SWA_K)]
        return jnp.dot(vt, e.astype(_BF16), preferred_element_type=_F32) * (1.0 / l)

    outs = []
    s_next = scores(0)
    for h in range(NB_HEADS):
        s = s_next
        if h + 1 < NB_HEADS:
            s_next = scores(h + 1)
        outs.append(attend(h, s))
        if h % 2:
            cols = slice((h - 1) * HEAD_DIM, (h + 1) * HEAD_DIM)
            o_ref[:, cols] = jnp.concatenate(outs[-2:], axis=0).T.astype(_BF16)


def _swa(qt, k, vt, sink):
    b, w, t = qt.shape
    assert k.shape[-1] == NB_KV_HEADS * HEAD_DIM == 2 * HEAD_DIM
    return pl.pallas_call(
        functools.partial(_swa_kernel, seq=t),
        grid=(b, t // SWA_Q),
        in_specs=[pl.BlockSpec(memory_space=pltpu.SMEM),
                  pl.BlockSpec((None, w, SWA_Q), lambda bi, n: (bi, 0, n)),
                  pl.BlockSpec((None, t, k.shape[-1]), lambda bi, n: (bi, 0, 0)),
                  pl.BlockSpec((None, vt.shape[1], t), lambda bi, n: (bi, 0, 0))],
        out_specs=pl.BlockSpec((None, SWA_Q, w), lambda bi, n: (bi, n, 0)),
        out_shape=jax.ShapeDtypeStruct((b, t, w), _BF16),
        compiler_params=pltpu.CompilerParams(
            dimension_semantics=("parallel", "arbitrary"), vmem_limit_bytes=VMEM_LIMIT),
        name="swa",
    )(sink, qt, k, vt)


def _merge_kernel(h_ref, ya_ref, yb_ref, ga_ref, gb_ref, wa_ref, wb_ref, wo_ref,
                  g2_ref, wg_ref, wu_ref, wd_ref, gf_ref, y_ref):
    ma = jnp.dot(ya_ref[...], wa_ref[...], preferred_element_type=_F32)
    mb = jnp.dot(yb_ref[...], wb_ref[...], preferred_element_type=_F32)
    merged = ga_ref[...].astype(_F32) * ma + gb_ref[...].astype(_F32) * mb
    h = h_ref[...] + jnp.dot(merged.astype(_BF16), wo_ref[...], preferred_element_type=_F32)
    hb = _rms(h, g2_ref[...]).astype(_BF16)
    h = h + 0.5 * _swiglu(hb, wg_ref, wu_ref, wd_ref)
    y_ref[...] = _rms(h, gf_ref[...])


def _merge(h2d, ya, yb, ga, gb, wa, wb, wo, g2, wg, wu, wd, gf):
    n, d = h2d.shape

    def tile(width):
        return pl.BlockSpec((TOKEN_TILE, width), lambda i: (i, 0))

    consts = (wa, wb, wo, g2, wg, wu, wd, gf)
    return pl.pallas_call(
        _merge_kernel,
        grid=(n // TOKEN_TILE,),
        in_specs=[tile(d), tile(ya.shape[-1]), tile(yb.shape[-1]), tile(d), tile(d)]
                 + [_const_spec(c.shape) for c in consts],
        out_specs=tile(d),
        out_shape=jax.ShapeDtypeStruct((n, d), _F32),
        compiler_params=pltpu.CompilerParams(dimension_semantics=("parallel",), vmem_limit_bytes=VMEM_LIMIT),
        name="merge_ffn2",
    )(h2d, ya, yb, ga, gb, *consts)


def kernel(x, ffn1_norm, ffn1_w_gate, ffn1_w_up, ffn1_w_down, mix_norm, w_in, na_rpb, sink_logit, w_branch_a, w_branch_b, w_out, ffn2_norm, ffn2_w_gate, ffn2_w_up, ffn2_w_down, final_norm):
    b, t, d = x.shape
    assert ffn1_norm.shape[0] == 1, "single-layer trunk: the final norm is fused into the layer's last kernel"
    rows = t // GRID_W
    cos_t, sin_t = _rope_tables(t)
    bf = lambda w: w[0].astype(_BF16)
    row = lambda g: g.reshape(1, d).astype(_F32)
    seq3 = lambda a: a.reshape(b, t, a.shape[-1])

    h = _ffn1(x.reshape(b * t, d), row(ffn1_norm), bf(ffn1_w_gate), bf(ffn1_w_up), bf(ffn1_w_down))
    qat, ka, vat, qbt, kb, vbt, ga, gb = _inproj(h, row(mix_norm), bf(w_in), cos_t, sin_t, t)
    ya = _natten(qat, seq3(ka), vat, _na_bias_tables(na_rpb[0].astype(_F32), rows))
    yb = _swa(qbt, seq3(kb), vbt, sink_logit[0].astype(_F32))
    y = _merge(h, ya.reshape(b * t, -1), yb.reshape(b * t, -1), ga, gb,
               bf(w_branch_a), bf(w_branch_b), bf(w_out),
               row(ffn2_norm), bf(ffn2_w_gate), bf(ffn2_w_up), bf(ffn2_w_down), row(final_norm))
    return y.reshape(b, t, d)
```

```python
import functools

import jax
import jax.numpy as jnp
import numpy as np
from jax import lax
from jax.experimental import pallas as pl
from jax.experimental.pallas import tpu as pltpu

D_MODEL = 1024
HEAD_DIM = 64
NA_HEADS = 8
NB_HEADS = 8
NB_KV_HEADS = 2
GRID_W = 64
NA_KH = 8
NA_KW = 16
WIN = 128
ROPE_THETA = 10000.0
D_FF = 2816
EPS = 1e-6

WIDTH_A = NA_HEADS * HEAD_DIM
WIDTH_BQ = NB_HEADS * HEAD_DIM
WIDTH_BKV = NB_KV_HEADS * HEAD_DIM
LOG2E = 1.4426950408889634
Q_SCALE = HEAD_DIM ** -0.5 * LOG2E
MASK_VALUE = -1e30
LANES = 128
BF16_SUBLANES = 16

TOKEN_TILE = 512
FF_CHUNKS = ((0, 1536), (1536, 2816))
NA_Q_ROWS = 4
NA_K_ROWS = NA_Q_ROWS + NA_KH
SWA_Q = 256
SWA_K = SWA_Q + 2 * WIN
SCORES_AHEAD = 4
VMEM_LIMIT = 56 * 1024 * 1024

_BF16 = jnp.bfloat16
_F32 = jnp.float32


def _rms(x, g):
    return x * lax.rsqrt(jnp.mean(x * x, axis=-1, keepdims=True) + EPS) * g


def _sigmoid(x):
    return 1.0 / (1.0 + jnp.exp(-x))


def _swiglu(xb, wg_ref, wu_ref, wd_ref):
    acc = None
    for lo, hi in FF_CHUNKS:
        g = jnp.dot(xb, wg_ref[:, lo:hi], preferred_element_type=_F32)
        u = jnp.dot(xb, wu_ref[:, lo:hi], preferred_element_type=_F32)
        a = (g * _sigmoid(g) * u).astype(_BF16)
        part = jnp.dot(a, wd_ref[lo:hi, :], preferred_element_type=_F32)
        acc = part if acc is None else acc + part
    return acc


def _const_spec(shape):
    return pl.BlockSpec(shape, lambda *_: (0,) * len(shape), pipeline_mode=pl.Buffered(1))


def _ffn1_kernel(x_ref, g_ref, wg_ref, wu_ref, wd_ref, h_ref):
    x = x_ref[...]
    xb = _rms(x, g_ref[...]).astype(_BF16)
    h_ref[...] = x + 0.5 * _swiglu(xb, wg_ref, wu_ref, wd_ref)


def _ffn1(x2d, gain, wg, wu, wd):
    n, d = x2d.shape
    tile = pl.BlockSpec((TOKEN_TILE, d), lambda i: (i, 0))
    return pl.pallas_call(
        _ffn1_kernel,
        grid=(n // TOKEN_TILE,),
        in_specs=[tile, _const_spec((1, d)), _const_spec(wg.shape), _const_spec(wu.shape), _const_spec(wd.shape)],
        out_specs=tile,
        out_shape=jax.ShapeDtypeStruct((n, d), _F32),
        compiler_params=pltpu.CompilerParams(dimension_semantics=("parallel",), vmem_limit_bytes=VMEM_LIMIT),
        name="ffn1",
    )(x2d, gain, wg, wu, wd)


_OFF_QA, _OFF_KA, _OFF_VA = 0, WIDTH_A, 2 * WIDTH_A
_OFF_QB = 3 * WIDTH_A
_OFF_KVB = _OFF_QB + WIDTH_BQ
_OFF_GA = _OFF_KVB + 2 * WIDTH_BKV
_OFF_GB = _OFF_GA + D_MODEL
D_IN = _OFF_GB + D_MODEL


def _rope(x, cos, sin_signed, first_half):
    w = x.shape[-1]
    half = HEAD_DIM // 2
    partner = jnp.where(first_half, pltpu.roll(x, w - half, 1), pltpu.roll(x, half, 1))
    return x * cos + partner * sin_signed


def _inproj_kernel(h_ref, g_ref, w_ref, cos_ref, sin_ref,
                   qat_ref, ka_ref, vat_ref, qbt_ref, kb_ref, vbt_ref, ga_ref, gb_ref):
    u = _rms(h_ref[...], g_ref[...]).astype(_BF16)

    def proj(lo, hi):
        return jnp.dot(u, w_ref[:, lo:hi], preferred_element_type=_F32)

    qat_ref[...] = (proj(_OFF_QA, _OFF_KA) * Q_SCALE).T.astype(_BF16)
    ka_ref[...] = proj(_OFF_KA, _OFF_VA).astype(_BF16)
    vat_ref[...] = proj(_OFF_VA, _OFF_QB).T.astype(_BF16)

    cos = cos_ref[...]
    sin = sin_ref[...]
    lane = lax.broadcasted_iota(jnp.int32, cos.shape, 1)
    first = (lane % HEAD_DIM) < (HEAD_DIM // 2)
    reps = WIDTH_BQ // cos.shape[-1]
    cos_q = jnp.concatenate([cos] * reps, axis=-1)
    sin_q = jnp.concatenate([sin] * reps, axis=-1)
    first_q = jnp.concatenate([first] * reps, axis=-1)
    qb = _rope(proj(_OFF_QB, _OFF_KVB), cos_q, sin_q, first_q)
    qbt_ref[...] = (qb * Q_SCALE).T.astype(_BF16)

    kv = proj(_OFF_KVB, _OFF_GA)
    kb_ref[...] = _rope(kv[:, :WIDTH_BKV], cos, sin, first).astype(_BF16)
    vbt_ref[...] = kv[:, WIDTH_BKV:].T.astype(_BF16)

    ga_ref[...] = _sigmoid(proj(_OFF_GA, _OFF_GB)).astype(_BF16)
    gb_ref[...] = _sigmoid(proj(_OFF_GB, D_IN)).astype(_BF16)


def _inproj(h2d, gain, w_in, cos_t, sin_t, seq):
    n, d = h2d.shape
    tiles_per_seq = seq // TOKEN_TILE

    def tile(width):
        return pl.BlockSpec((TOKEN_TILE, width), lambda i: (i, 0))

    table = pl.BlockSpec((TOKEN_TILE, cos_t.shape[-1]), lambda i: (i % tiles_per_seq, 0))
    def tposed(width):
        return pl.BlockSpec((None, width, TOKEN_TILE), lambda i: (i // tiles_per_seq, 0, i % tiles_per_seq))

    outs = [
        (True, WIDTH_A), (False, WIDTH_A), (True, WIDTH_A), (True, WIDTH_BQ), (False, WIDTH_BKV), (True, WIDTH_BKV),
        (False, D_MODEL), (False, D_MODEL)]
    return pl.pallas_call(
        _inproj_kernel,
        grid=(n // TOKEN_TILE,),
        in_specs=[tile(d), _const_spec((1, d)), _const_spec(w_in.shape), table, table],
        out_specs=[tposed(w) if t else tile(w) for t, w in outs],
        out_shape=[jax.ShapeDtypeStruct((n // seq, w, seq) if t else (n, w), _BF16) for t, w in outs],
        compiler_params=pltpu.CompilerParams(dimension_semantics=("parallel",), vmem_limit_bytes=VMEM_LIMIT),
        name="inproj",
    )(h2d, gain, w_in, cos_t, sin_t)


def _rope_tables(seq):
    half = HEAD_DIM // 2
    inv = ROPE_THETA ** (-jnp.arange(half, dtype=_F32) / half)
    ang = jnp.arange(seq).astype(_F32)[:, None] * inv[None, :]
    cos, sin = jnp.cos(ang), jnp.sin(ang)
    cos_h = jnp.concatenate([cos, cos], axis=-1)
    sin_h = jnp.concatenate([-sin, sin], axis=-1)
    return jnp.tile(cos_h, (1, 2)), jnp.tile(sin_h, (1, 2))


def _na_bias_tables(rpb, rows):
    heads, n_dr, n_dc = rpb.shape
    span = 2 * GRID_W
    ring = jnp.concatenate(
        [rpb[..., NA_KW - 1:], jnp.zeros((heads, n_dr, span - n_dc), rpb.dtype), rpb[..., :NA_KW - 1]], axis=-1)
    skew = jnp.tile(ring, (1, 1, GRID_W))[..., :GRID_W * (span - 1)].reshape(heads, n_dr, GRID_W, span - 1)
    c = np.arange(GRID_W)
    cs = np.clip(c - NA_KW // 2, 0, GRID_W - NA_KW)
    col_ok = (c[None, :] >= cs[:, None]) & (c[None, :] < cs[:, None] + NA_KW)
    toe = jnp.where(col_ok, skew[..., :GRID_W], MASK_VALUE)
    toe = jnp.swapaxes(toe, -1, -2)
    masked = jnp.full((heads, GRID_W, GRID_W), MASK_VALUE, rpb.dtype)

    steps = rows // NA_Q_ROWS
    tables = []
    for step in (0, 1, steps - 1):
        key_row0 = int(np.clip(step * NA_Q_ROWS - NA_KH // 2, 0, rows - NA_K_ROWS))
        key_blocks = []
        for kr in key_row0 + np.arange(NA_K_ROWS):
            query_blocks = []
            for qr in step * NA_Q_ROWS + np.arange(NA_Q_ROWS):
                rs = int(np.clip(qr - NA_KH // 2, 0, rows - NA_KH))
                inside = rs <= kr < rs + NA_KH
                query_blocks.append(toe[:, kr - qr + NA_KH - 1] if inside else masked)
            key_blocks.append(jnp.concatenate(query_blocks, axis=-1))
        tables.append(jnp.concatenate(key_blocks, axis=1))
    return jnp.stack(tables)


def _softmax_pv(s, vt, key_ranges, sink=None):
    nk, nq = s.shape
    probs, maxes = [], []
    for t, (lo, hi) in enumerate(key_ranges):
        band = s[lo:hi, t * LANES:(t + 1) * LANES]
        m = jnp.max(band, axis=0, keepdims=True)
        if sink is not None:
            m = jnp.maximum(m, sink)
        e = jnp.exp2(band - m).astype(_BF16)
        pads = [jnp.zeros((n, LANES), _BF16) for n in (lo, nk - hi)]
        probs.append(jnp.concatenate([x for x in (pads[0], e, pads[1]) if x.shape[0]], axis=0))
        maxes.append(m)
    p = jnp.concatenate(probs, axis=1)
    vt_ones = jnp.concatenate([vt, jnp.ones((BF16_SUBLANES, nk), _BF16)], axis=0)
    ov = jnp.dot(vt_ones, p, preferred_element_type=_F32)
    l = ov[HEAD_DIM:HEAD_DIM + 1, :]
    if sink is not None:
        l = l + jnp.exp2(sink - jnp.concatenate(maxes, axis=1))
    return ov[:HEAD_DIM, :] * (1.0 / l)


def _natten_kernel(qt_ref, k_ref, vt_ref, bias_ref, o_ref, *, steps):
    a = pl.program_id(1)
    blk = NA_Q_ROWS * GRID_W
    nk = NA_K_ROWS * GRID_W
    start = pl.multiple_of(jnp.clip(a - 1, 0, steps - 3) * blk, blk)
    pair = 2 * HEAD_DIM
    upper = lax.broadcasted_iota(jnp.int32, (pair, blk), 0) >= HEAD_DIM

    def scores(h):
        rows = slice(h // 2 * pair, (h // 2 + 1) * pair)
        k2 = k_ref[pl.ds(start, nk), rows]
        q2 = qt_ref[rows, :]
        qh = jnp.where(upper if h % 2 else ~upper, q2, jnp.zeros_like(q2))
        return jnp.dot(k2, qh, preferred_element_type=_F32) + bias_ref[h]

    def run(key_ranges):
        outs = []
        pending = [scores(h) for h in range(SCORES_AHEAD)]
        for h in range(NA_HEADS):
            if h + SCORES_AHEAD < NA_HEADS:
                pending.append(scores(h + SCORES_AHEAD))
            s = pending.pop(0)
            vt = vt_ref[h * HEAD_DIM:(h + 1) * HEAD_DIM, pl.ds(start, nk)]
            outs.append(_softmax_pv(s, vt, key_ranges))
            if h % 2:
                rows = slice(h // 2 * pair, (h // 2 + 1) * pair)
                o_ref[:, rows] = jnp.concatenate(outs[-2:], axis=0).T.astype(_BF16)

    rows_per_tile = LANES // GRID_W
    interior_ranges = tuple((t * rows_per_tile * GRID_W, (t * rows_per_tile + rows_per_tile + NA_KH - 1) * GRID_W)
                            for t in range(blk // LANES))
    interior = jnp.logical_and(a >= 1, a <= steps - 2)
    pl.when(interior)(lambda: run(interior_ranges))
    pl.when(jnp.logical_not(interior))(lambda: run(((0, nk),) * (blk // LANES)))


def _natten(qt, k, vt, bias):
    b, t, w = k.shape
    blk = NA_Q_ROWS * GRID_W
    steps = t // blk
    assert NA_K_ROWS * GRID_W == 3 * blk and steps >= 3

    def cfg(bi, a):
        return (jnp.where(a == 0, 0, jnp.where(a == steps - 1, 2, 1)), 0, 0, 0)

    qtspec = pl.BlockSpec((None, w, blk), lambda bi, a: (bi, 0, a))
    kspec = pl.BlockSpec((None, t, w), lambda bi, a: (bi, 0, 0))
    vtspec = pl.BlockSpec((None, w, t), lambda bi, a: (bi, 0, 0))
    bspec = pl.BlockSpec((None,) + bias.shape[1:], cfg)
    return pl.pallas_call(
        functools.partial(_natten_kernel, steps=steps),
        grid=(b, steps),
        in_specs=[qtspec, kspec, vtspec, bspec],
        out_specs=pl.BlockSpec((None, blk, w), lambda bi, a: (bi, a, 0)),
        out_shape=jax.ShapeDtypeStruct((b, t, w), _BF16),
        compiler_params=pltpu.CompilerParams(
            dimension_semantics=("parallel", "arbitrary"), vmem_limit_bytes=VMEM_LIMIT),
        name="natten",
    )(qt, k, vt, bias)


def _swa_kernel(sink_ref, qt_ref, k_ref, vt_ref, o_ref, *, seq):
    n = pl.program_id(1)
    start = pl.multiple_of(jnp.clip(n * SWA_Q - WIN, 0, seq - SWA_K), WIN)
    kpos = start + lax.broadcasted_iota(jnp.int32, (SWA_K, SWA_Q), 0)
    qpos = n * SWA_Q + lax.broadcasted_iota(jnp.int32, (SWA_K, SWA_Q), 1)
    mask_bias = jnp.where(jnp.abs(qpos - kpos) <= WIN, 0.0, MASK_VALUE).astype(_F32)
    group = NB_HEADS // NB_KV_HEADS
    k2 = k_ref[pl.ds(start, SWA_K), :]
    zeros = jnp.zeros((HEAD_DIM, SWA_Q), _BF16)

    def scores(h):
        qh = qt_ref[h * HEAD_DIM:(h + 1) * HEAD_DIM, :]
        qh = jnp.concatenate([qh, zeros] if h // group == 0 else [zeros, qh], axis=0)
        return jnp.dot(k2, qh, preferred_element_type=_F32) + mask_bias

    def run(key_ranges):
        outs = []
        pending = [scores(h) for h in range(SCORES_AHEAD)]
        for h in range(NB_HEADS):
            if h + SCORES_AHEAD < NB_HEADS:
                pending.append(scores(h + SCORES_AHEAD))
            s = pending.pop(0)
            j = h // group
            vt = vt_ref[j * HEAD_DIM:(j + 1) * HEAD_DIM, pl.ds(start, SWA_K)]
            outs.append(_softmax_pv(s, vt, key_ranges, sink=sink_ref[h] * LOG2E))
            if h % 2:
                cols = slice((h - 1) * HEAD_DIM, (h + 1) * HEAD_DIM)
                o_ref[:, cols] = jnp.concatenate(outs[-2:], axis=0).T.astype(_BF16)

    n_tiles = SWA_Q // LANES
    interior_ranges = tuple((t * LANES, t * LANES + LANES + 2 * WIN) for t in range(n_tiles))
    interior = jnp.logical_and(n * SWA_Q - WIN >= 0, n * SWA_Q - WIN <= seq - SWA_K)
    pl.when(interior)(lambda: run(interior_ranges))
    pl.when(jnp.logical_not(interior))(lambda: run(((0, SWA_K),) * n_tiles))


def _swa(qt, k, vt, sink):
    b, w, t = qt.shape
    assert k.shape[-1] == NB_KV_HEADS * HEAD_DIM == 2 * HEAD_DIM
    return pl.pallas_call(
        functools.partial(_swa_kernel, seq=t),
        grid=(b, t // SWA_Q),
        in_specs=[pl.BlockSpec(memory_space=pltpu.SMEM),
                  pl.BlockSpec((None, w, SWA_Q), lambda bi, n: (bi, 0, n)),
                  pl.BlockSpec((None, t, k.shape[-1]), lambda bi, n: (bi, 0, 0)),
                  pl.BlockSpec((None, vt.shape[1], t), lambda bi, n: (bi, 0, 0))],
        out_specs=pl.BlockSpec((None, SWA_Q, w), lambda bi, n: (bi, n, 0)),
        out_shape=jax.ShapeDtypeStruct((b, t, w), _BF16),
        compiler_params=pltpu.CompilerParams(
            dimension_semantics=("parallel", "arbitrary"), vmem_limit_bytes=VMEM_LIMIT),
        name="swa",
    )(sink, qt, k, vt)


def _merge_kernel(h_ref, ya_ref, yb_ref, ga_ref, gb_ref, wa_ref, wb_ref, wo_ref,
                  g2_ref, wg_ref, wu_ref, wd_ref, gf_ref, y_ref):
    ma = jnp.dot(ya_ref[...], wa_ref[...], preferred_element_type=_F32)
    mb = jnp.dot(yb_ref[...], wb_ref[...], preferred_element_type=_F32)
    merged = ga_ref[...].astype(_F32) * ma + gb_ref[...].astype(_F32) * mb
    h = h_ref[...] + jnp.dot(merged.astype(_BF16), wo_ref[...], preferred_element_type=_F32)
    hb = _rms(h, g2_ref[...]).astype(_BF16)
    h = h + 0.5 * _swiglu(hb, wg_ref, wu_ref, wd_ref)
    y_ref[...] = _rms(h, gf_ref[...])


def _merge(h2d, ya, yb, ga, gb, wa, wb, wo, g2, wg, wu, wd, gf):
    n, d = h2d.shape

    def tile(width):
        return pl.BlockSpec((TOKEN_TILE, width), lambda i: (i, 0))

    consts = (wa, wb, wo, g2, wg, wu, wd, gf)
    return pl.pallas_call(
        _merge_kernel,
        grid=(n // TOKEN_TILE,),
        in_specs=[tile(d), tile(ya.shape[-1]), tile(yb.shape[-1]), tile(d), tile(d)]
                 + [_const_spec(c.shape) for c in consts],
        out_specs=tile(d),
        out_shape=jax.ShapeDtypeStruct((n, d), _F32),
        compiler_params=pltpu.CompilerParams(dimension_semantics=("parallel",), vmem_limit_bytes=VMEM_LIMIT),
        name="merge_ffn2",
    )(h2d, ya, yb, ga, gb, *consts)


def kernel(x, ffn1_norm, ffn1_w_gate, ffn1_w_up, ffn1_w_down, mix_norm, w_in, na_rpb, sink_logit, w_branch_a, w_branch_b, w_out, ffn2_norm, ffn2_w_gate, ffn2_w_up, ffn2_w_down, final_norm):
    b, t, d = x.shape
    assert ffn1_norm.shape[0] == 1, "single-layer trunk: the final norm is fused into the layer's last kernel"
    rows = t // GRID_W
    cos_t, sin_t = _rope_tables(t)
    bf = lambda w: w[0].astype(_BF16)
    row = lambda g: g.reshape(1, d).astype(_F32)
    seq3 = lambda a: a.reshape(b, t, a.shape[-1])

    h = _ffn1(x.reshape(b * t, d), row(ffn1_norm), bf(ffn1_w_gate), bf(ffn1_w_up), bf(ffn1_w_down))
    qat, ka, vat, qbt, kb, vbt, ga, gb = _inproj(h, row(mix_norm), bf(w_in), cos_t, sin_t, t)
    ya = _natten(qat, seq3(ka), vat, _na_bias_tables(na_rpb[0].astype(_F32) * LOG2E, rows))
    yb = _swa(qbt, seq3(kb), vbt, sink_logit[0].astype(_F32))
    y = _merge(h, ya.reshape(b * t, -1), yb.reshape(b * t, -1), ga, gb,
               bf(w_branch_a), bf(w_branch_b), bf(w_out),
               row(ffn2_norm), bf(ffn2_w_gate), bf(ffn2_w_up), bf(ffn2_w_down), row(final_norm))
    return y.reshape(b, t, d)
```

```python
import functools

import jax
import jax.numpy as jnp
import numpy as np
from jax import lax
from jax.experimental import pallas as pl
from jax.experimental.pallas import tpu as pltpu

D_MODEL = 1024
HEAD_DIM = 64
NA_HEADS = 8
NB_HEADS = 8
NB_KV_HEADS = 2
GRID_W = 64
NA_KH = 8
NA_KW = 16
WIN = 128
ROPE_THETA = 10000.0
D_FF = 2816
EPS = 1e-6

WIDTH_A = NA_HEADS * HEAD_DIM
WIDTH_BQ = NB_HEADS * HEAD_DIM
WIDTH_BKV = NB_KV_HEADS * HEAD_DIM
LOG2E = 1.4426950408889634
Q_SCALE = HEAD_DIM ** -0.5 * LOG2E
MASK_VALUE = -1e30
LANES = 128
BF16_SUBLANES = 16

TOKEN_TILE = 512
DENSE_SUBSTEPS = 2
MERGE_SUB_TILES = 2
FF_CHUNKS = ((0, 1536), (1536, 2816))
NA_Q_ROWS = 4
NA_K_ROWS = NA_Q_ROWS + NA_KH
SWA_Q = 256
SWA_K = SWA_Q + 2 * WIN
ATTN_SUBSTEPS = 4
SCORES_AHEAD = 4
VMEM_LIMIT = 56 * 1024 * 1024

_BF16 = jnp.bfloat16
_F32 = jnp.float32


def _rms(x, g):
    return x * lax.rsqrt(jnp.mean(x * x, axis=-1, keepdims=True) + EPS) * g


def _sigmoid(x):
    return 1.0 / (1.0 + jnp.exp(-x))


def _swiglu(xb, wg_ref, wu_ref, wd_ref):
    acc = None
    for lo, hi in FF_CHUNKS:
        g = jnp.dot(xb, wg_ref[:, lo:hi], preferred_element_type=_F32)
        u = jnp.dot(xb, wu_ref[:, lo:hi], preferred_element_type=_F32)
        a = (g * _sigmoid(g) * u).astype(_BF16)
        part = jnp.dot(a, wd_ref[lo:hi, :], preferred_element_type=_F32)
        acc = part if acc is None else acc + part
    return acc


def _const_spec(shape):
    return pl.BlockSpec(shape, lambda *_: (0,) * len(shape), pipeline_mode=pl.Buffered(1))


def _for_each_substep(count, size, body):
    def step(i, carry):
        body(pl.multiple_of(i * size, size))
        return carry
    lax.fori_loop(0, count, step, 0)


def _ffn1_kernel(x_ref, g_ref, wg_ref, wu_ref, wd_ref, h_ref):
    def tile(off):
        rows = pl.ds(off, TOKEN_TILE)
        x = x_ref[rows, :]
        xb = _rms(x, g_ref[...]).astype(_BF16)
        h_ref[rows, :] = x + 0.5 * _swiglu(xb, wg_ref, wu_ref, wd_ref)

    _for_each_substep(DENSE_SUBSTEPS, TOKEN_TILE, tile)


def _ffn1(x2d, gain, wg, wu, wd):
    n, d = x2d.shape
    block = TOKEN_TILE * DENSE_SUBSTEPS
    tile = pl.BlockSpec((block, d), lambda i: (i, 0))
    return pl.pallas_call(
        _ffn1_kernel,
        grid=(n // block,),
        in_specs=[tile, _const_spec((1, d)), _const_spec(wg.shape), _const_spec(wu.shape), _const_spec(wd.shape)],
        out_specs=tile,
        out_shape=jax.ShapeDtypeStruct((n, d), _F32),
        compiler_params=pltpu.CompilerParams(dimension_semantics=("parallel",), vmem_limit_bytes=VMEM_LIMIT),
        name="ffn1",
    )(x2d, gain, wg, wu, wd)


_OFF_QA, _OFF_KA, _OFF_VA = 0, WIDTH_A, 2 * WIDTH_A
_OFF_QB = 3 * WIDTH_A
_OFF_KVB = _OFF_QB + WIDTH_BQ
_OFF_GA = _OFF_KVB + 2 * WIDTH_BKV
_OFF_GB = _OFF_GA + D_MODEL
D_IN = _OFF_GB + D_MODEL


def _rope(x, cos, sin_signed, first_half):
    w = x.shape[-1]
    half = HEAD_DIM // 2
    partner = jnp.where(first_half, pltpu.roll(x, w - half, 1), pltpu.roll(x, half, 1))
    return x * cos + partner * sin_signed


def _inproj_kernel(h_ref, g_ref, w_ref, cos_ref, sin_ref,
                   qat_ref, ka_ref, vat_ref, qbt_ref, kb_ref, vbt_ref, ga_ref, gb_ref):
    def tile(off):
        rows = pl.ds(off, TOKEN_TILE)
        u = _rms(h_ref[rows, :], g_ref[...]).astype(_BF16)

        def proj(lo, hi):
            return jnp.dot(u, w_ref[:, lo:hi], preferred_element_type=_F32)

        qat_ref[:, rows] = (proj(_OFF_QA, _OFF_KA) * Q_SCALE).T.astype(_BF16)
        ka_ref[rows, :] = proj(_OFF_KA, _OFF_VA).astype(_BF16)
        vat_ref[:, rows] = proj(_OFF_VA, _OFF_QB).T.astype(_BF16)

        cos = cos_ref[rows, :]
        sin = sin_ref[rows, :]
        lane = lax.broadcasted_iota(jnp.int32, cos.shape, 1)
        first = (lane % HEAD_DIM) < (HEAD_DIM // 2)
        reps = WIDTH_BQ // cos.shape[-1]
        cos_q = jnp.concatenate([cos] * reps, axis=-1)
        sin_q = jnp.concatenate([sin] * reps, axis=-1)
        first_q = jnp.concatenate([first] * reps, axis=-1)
        qb = _rope(proj(_OFF_QB, _OFF_KVB), cos_q, sin_q, first_q)
        qbt_ref[:, rows] = (qb * Q_SCALE).T.astype(_BF16)

        kv = proj(_OFF_KVB, _OFF_GA)
        kb_ref[rows, :] = _rope(kv[:, :WIDTH_BKV], cos, sin, first).astype(_BF16)
        vbt_ref[:, rows] = kv[:, WIDTH_BKV:].T.astype(_BF16)

        ga_ref[rows, :] = _sigmoid(proj(_OFF_GA, _OFF_GB)).astype(_BF16)
        gb_ref[rows, :] = _sigmoid(proj(_OFF_GB, D_IN)).astype(_BF16)

    _for_each_substep(DENSE_SUBSTEPS, TOKEN_TILE, tile)


def _inproj(h2d, gain, w_in, cos_t, sin_t, seq):
    n, d = h2d.shape
    block = TOKEN_TILE * DENSE_SUBSTEPS
    blocks_per_seq = seq // block

    def tile(width):
        return pl.BlockSpec((block, width), lambda i: (i, 0))

    def tposed(width):
        return pl.BlockSpec((None, width, block), lambda i: (i // blocks_per_seq, 0, i % blocks_per_seq))

    table = pl.BlockSpec((block, cos_t.shape[-1]), lambda i: (i % blocks_per_seq, 0))
    outs = [
        (True, WIDTH_A), (False, WIDTH_A), (True, WIDTH_A), (True, WIDTH_BQ), (False, WIDTH_BKV), (True, WIDTH_BKV),
        (False, D_MODEL), (False, D_MODEL)]
    return pl.pallas_call(
        _inproj_kernel,
        grid=(n // block,),
        in_specs=[tile(d), _const_spec((1, d)), _const_spec(w_in.shape), table, table],
        out_specs=[tposed(w) if t else tile(w) for t, w in outs],
        out_shape=[jax.ShapeDtypeStruct((n // seq, w, seq) if t else (n, w), _BF16) for t, w in outs],
        compiler_params=pltpu.CompilerParams(dimension_semantics=("parallel",), vmem_limit_bytes=VMEM_LIMIT),
        name="inproj",
    )(h2d, gain, w_in, cos_t, sin_t)


def _rope_tables(seq):
    half = HEAD_DIM // 2
    inv = ROPE_THETA ** (-jnp.arange(half, dtype=_F32) / half)
    ang = jnp.arange(seq).astype(_F32)[:, None] * inv[None, :]
    cos, sin = jnp.cos(ang), jnp.sin(ang)
    cos_h = jnp.concatenate([cos, cos], axis=-1)
    sin_h = jnp.concatenate([-sin, sin], axis=-1)
    return jnp.tile(cos_h, (1, 2)), jnp.tile(sin_h, (1, 2))


def _softmax_pv(s, vt, key_ranges, sink=None):
    nk, nq = s.shape
    probs, maxes = [], []
    for t, (lo, hi) in enumerate(key_ranges):
        band = s[lo:hi, t * LANES:(t + 1) * LANES]
        m = jnp.max(band, axis=0, keepdims=True)
        if sink is not None:
            m = jnp.maximum(m, sink)
        e = jnp.exp2(band - m).astype(_BF16)
        pads = [jnp.zeros((n, LANES), _BF16) for n in (lo, nk - hi)]
        probs.append(jnp.concatenate([x for x in (pads[0], e, pads[1]) if x.shape[0]], axis=0))
        maxes.append(m)
    p = jnp.concatenate(probs, axis=1)
    vt_ones = jnp.concatenate([vt, jnp.ones((BF16_SUBLANES, nk), _BF16)], axis=0)
    ov = jnp.dot(vt_ones, p, preferred_element_type=_F32)
    l = ov[HEAD_DIM:HEAD_DIM + 1, :]
    if sink is not None:
        l = l + jnp.exp2(sink - jnp.concatenate(maxes, axis=1))
    return ov[:HEAD_DIM, :] * (1.0 / l)


def _attend_heads(n_heads, scores, values, key_ranges, store_pair, sink=None):
    pending = [scores(h) for h in range(SCORES_AHEAD)]
    outs = []
    for h in range(n_heads):
        if h + SCORES_AHEAD < n_heads:
            pending.append(scores(h + SCORES_AHEAD))
        outs.append(_softmax_pv(pending.pop(0), values(h), key_ranges, None if sink is None else sink(h)))
        if h % 2:
            store_pair(h - 1, jnp.concatenate(outs[-2:], axis=0))


def _na_bias_tables(rpb, rows):
    heads, n_dr, n_dc = rpb.shape
    span = 2 * GRID_W
    ring = jnp.concatenate(
        [rpb[..., NA_KW - 1:], jnp.zeros((heads, n_dr, span - n_dc), rpb.dtype), rpb[..., :NA_KW - 1]], axis=-1)
    skew = jnp.tile(ring, (1, 1, GRID_W))[..., :GRID_W * (span - 1)].reshape(heads, n_dr, GRID_W, span - 1)
    c = np.arange(GRID_W)
    cs = np.clip(c - NA_KW // 2, 0, GRID_W - NA_KW)
    col_ok = (c[None, :] >= cs[:, None]) & (c[None, :] < cs[:, None] + NA_KW)
    toe = jnp.where(col_ok, skew[..., :GRID_W], MASK_VALUE)
    toe = jnp.swapaxes(toe, -1, -2)
    masked = jnp.full((heads, GRID_W, GRID_W), MASK_VALUE, rpb.dtype)

    groups = rows // NA_Q_ROWS
    tables = []
    for group in (0, 1, groups - 1):
        key_row0 = int(np.clip(group * NA_Q_ROWS - NA_KH // 2, 0, rows - NA_K_ROWS))
        key_blocks = []
        for kr in key_row0 + np.arange(NA_K_ROWS):
            query_blocks = []
            for qr in group * NA_Q_ROWS + np.arange(NA_Q_ROWS):
                rs = int(np.clip(qr - NA_KH // 2, 0, rows - NA_KH))
                inside = rs <= kr < rs + NA_KH
                query_blocks.append(toe[:, kr - qr + NA_KH - 1] if inside else masked)
            key_blocks.append(jnp.concatenate(query_blocks, axis=-1))
        tables.append(jnp.concatenate(key_blocks, axis=1))
    return jnp.stack(tables)


def _natten_kernel(qt_ref, k_ref, vt_ref, bias_mid_ref, bias_edge_ref, o_ref, *, groups):
    blk = NA_Q_ROWS * GRID_W
    nk = NA_K_ROWS * GRID_W
    pair = 2 * HEAD_DIM
    upper = lax.broadcasted_iota(jnp.int32, (pair, blk), 0) >= HEAD_DIM

    def substep(qoff):
        a = pl.program_id(1) * ATTN_SUBSTEPS + qoff // blk
        start = pl.multiple_of(jnp.clip(a - 1, 0, groups - 3) * blk, blk)

        def run(key_ranges, bias_ref):
            def scores(h):
                rows = slice(h // 2 * pair, (h // 2 + 1) * pair)
                k2 = k_ref[pl.ds(start, nk), rows]
                q2 = qt_ref[rows, pl.ds(qoff, blk)]
                qh = jnp.where(upper if h % 2 else ~upper, q2, jnp.zeros_like(q2))
                return jnp.dot(k2, qh, preferred_element_type=_F32) + bias_ref[h]

            def values(h):
                return vt_ref[h * HEAD_DIM:(h + 1) * HEAD_DIM, pl.ds(start, nk)]

            def store_pair(h, o2):
                o_ref[pl.ds(qoff, blk), h * HEAD_DIM:(h + 2) * HEAD_DIM] = o2.T.astype(_BF16)

            _attend_heads(NA_HEADS, scores, values, key_ranges, store_pair)

        rows_per_tile = LANES // GRID_W
        mid_ranges = tuple((t * rows_per_tile * GRID_W, (t * rows_per_tile + rows_per_tile + NA_KH - 1) * GRID_W)
                           for t in range(blk // LANES))
        mid = jnp.logical_and(a >= 1, a <= groups - 2)
        pl.when(mid)(lambda: run(mid_ranges, bias_mid_ref))
        pl.when(jnp.logical_not(mid))(lambda: run(((0, nk),) * (blk // LANES), bias_edge_ref))

    _for_each_substep(ATTN_SUBSTEPS, blk, substep)


def _natten(qt, k, vt, bias):
    b, t, w = k.shape
    blk = NA_Q_ROWS * GRID_W
    groups = t // blk
    steps = groups // ATTN_SUBSTEPS
    assert NA_K_ROWS * GRID_W == 3 * blk and groups >= 3 and steps * ATTN_SUBSTEPS == groups
    bias_mid, bias_edge = bias[1], bias[0::2]
    return pl.pallas_call(
        functools.partial(_natten_kernel, groups=groups),
        grid=(b, steps),
        in_specs=[pl.BlockSpec((None, w, ATTN_SUBSTEPS * blk), lambda bi, s: (bi, 0, s)),
                  pl.BlockSpec((None, t, w), lambda bi, s: (bi, 0, 0)),
                  pl.BlockSpec((None, w, t), lambda bi, s: (bi, 0, 0)),
                  _const_spec(bias_mid.shape),
                  pl.BlockSpec((None,) + bias_edge.shape[1:], lambda bi, s: (s // (steps - 1), 0, 0, 0),
                               pipeline_mode=pl.Buffered(1))],
        out_specs=pl.BlockSpec((None, ATTN_SUBSTEPS * blk, w), lambda bi, s: (bi, s, 0)),
        out_shape=jax.ShapeDtypeStruct((b, t, w), _BF16),
        compiler_params=pltpu.CompilerParams(
            dimension_semantics=("parallel", "arbitrary"), vmem_limit_bytes=VMEM_LIMIT),
        name="natten",
    )(qt, k, vt, bias_mid, bias_edge)


def _swa_mask(key_pos0, query_pos0, xp=jnp):
    iota = lax.broadcasted_iota if xp is jnp else (lambda dt, shape, dim: np.indices(shape)[dim])
    kpos = key_pos0 + iota(jnp.int32, (SWA_K, SWA_Q), 0)
    qpos = query_pos0 + iota(jnp.int32, (SWA_K, SWA_Q), 1)
    return xp.where(xp.abs(qpos - kpos) <= WIN, 0.0, MASK_VALUE).astype(xp.float32)


def _swa_kernel(sink_ref, qt_ref, k_ref, vt_ref, mask_mid_ref, o_ref, *, seq):
    group = NB_HEADS // NB_KV_HEADS
    zeros = jnp.zeros((HEAD_DIM, SWA_Q), _BF16)

    def substep(qoff):
        q0 = pl.program_id(1) * (ATTN_SUBSTEPS * SWA_Q) + qoff
        start = pl.multiple_of(jnp.clip(q0 - WIN, 0, seq - SWA_K), WIN)

        def run(key_ranges, mask):
            k2 = k_ref[pl.ds(start, SWA_K), :]

            def scores(h):
                qh = qt_ref[h * HEAD_DIM:(h + 1) * HEAD_DIM, pl.ds(qoff, SWA_Q)]
                qh = jnp.concatenate([qh, zeros] if h // group == 0 else [zeros, qh], axis=0)
                return jnp.dot(k2, qh, preferred_element_type=_F32) + mask

            def values(h):
                j = h // group
                return vt_ref[j * HEAD_DIM:(j + 1) * HEAD_DIM, pl.ds(start, SWA_K)]

            def store_pair(h, o2):
                o_ref[pl.ds(qoff, SWA_Q), h * HEAD_DIM:(h + 2) * HEAD_DIM] = o2.T.astype(_BF16)

            _attend_heads(NB_HEADS, scores, values, key_ranges, store_pair, sink=lambda h: sink_ref[h] * LOG2E)

        n_tiles = SWA_Q // LANES
        mid_ranges = tuple((t * LANES, t * LANES + LANES + 2 * WIN) for t in range(n_tiles))
        mid = jnp.logical_and(q0 - WIN >= 0, q0 - WIN <= seq - SWA_K)
        pl.when(mid)(lambda: run(mid_ranges, mask_mid_ref[...]))
        pl.when(jnp.logical_not(mid))(lambda: run(((0, SWA_K),) * n_tiles, _swa_mask(start, q0)))

    _for_each_substep(ATTN_SUBSTEPS, SWA_Q, substep)


def _swa(qt, k, vt, sink):
    b, w, t = qt.shape
    assert k.shape[-1] == NB_KV_HEADS * HEAD_DIM == 2 * HEAD_DIM
    block = ATTN_SUBSTEPS * SWA_Q
    mask_mid = jnp.asarray(_swa_mask(0, WIN, xp=np))
    return pl.pallas_call(
        functools.partial(_swa_kernel, seq=t),
        grid=(b, t // block),
        in_specs=[pl.BlockSpec(memory_space=pltpu.SMEM),
                  pl.BlockSpec((None, w, block), lambda bi, n: (bi, 0, n)),
                  pl.BlockSpec((None, t, k.shape[-1]), lambda bi, n: (bi, 0, 0)),
                  pl.BlockSpec((None, vt.shape[1], t), lambda bi, n: (bi, 0, 0)),
                  _const_spec(mask_mid.shape)],
        out_specs=pl.BlockSpec((None, block, w), lambda bi, n: (bi, n, 0)),
        out_shape=jax.ShapeDtypeStruct((b, t, w), _BF16),
        compiler_params=pltpu.CompilerParams(
            dimension_semantics=("parallel", "arbitrary"), vmem_limit_bytes=VMEM_LIMIT),
        name="swa",
    )(sink, qt, k, vt, mask_mid)


def _merge_kernel(h_ref, ya_ref, yb_ref, ga_ref, gb_ref, wa_ref, wb_ref, wo_ref,
                  g2_ref, wg_ref, wu_ref, wd_ref, gf_ref, y_ref):
    sub = TOKEN_TILE // MERGE_SUB_TILES
    hs = []
    for i in range(MERGE_SUB_TILES):
        r = slice(i * sub, (i + 1) * sub)
        ma = jnp.dot(ya_ref[r, :], wa_ref[...], preferred_element_type=_F32)
        mb = jnp.dot(yb_ref[r, :], wb_ref[...], preferred_element_type=_F32)
        merged = ga_ref[r, :].astype(_F32) * ma + gb_ref[r, :].astype(_F32) * mb
        hs.append(h_ref[r, :] + jnp.dot(merged.astype(_BF16), wo_ref[...], preferred_element_type=_F32))
    for i, h in enumerate(hs):
        hb = _rms(h, g2_ref[...]).astype(_BF16)
        h = h + 0.5 * _swiglu(hb, wg_ref, wu_ref, wd_ref)
        y_ref[i * sub:(i + 1) * sub, :] = _rms(h, gf_ref[...])


def _merge(h2d, ya, yb, ga, gb, wa, wb, wo, g2, wg, wu, wd, gf):
    n, d = h2d.shape

    def tile(width):
        return pl.BlockSpec((TOKEN_TILE, width), lambda i: (i, 0))

    consts = (wa, wb, wo, g2, wg, wu, wd, gf)
    return pl.pallas_call(
        _merge_kernel,
        grid=(n // TOKEN_TILE,),
        in_specs=[tile(d), tile(ya.shape[-1]), tile(yb.shape[-1]), tile(d), tile(d)]
                 + [_const_spec(c.shape) for c in consts],
        out_specs=tile(d),
        out_shape=jax.ShapeDtypeStruct((n, d), _F32),
        compiler_params=pltpu.CompilerParams(dimension_semantics=("parallel",), vmem_limit_bytes=VMEM_LIMIT),
        name="merge_ffn2",
    )(h2d, ya, yb, ga, gb, *consts)


def kernel(x, ffn1_norm, ffn1_w_gate, ffn1_w_up, ffn1_w_down, mix_norm, w_in, na_rpb, sink_logit, w_branch_a, w_branch_b, w_out, ffn2_norm, ffn2_w_gate, ffn2_w_up, ffn2_w_down, final_norm):
    b, t, d = x.shape
    assert ffn1_norm.shape[0] == 1, "single-layer trunk: the final norm is fused into the layer's last kernel"
    rows = t // GRID_W
    cos_t, sin_t = _rope_tables(t)
    bf = lambda w: w[0].astype(_BF16)
    row = lambda g: g.reshape(1, d).astype(_F32)
    seq3 = lambda a: a.reshape(b, t, a.shape[-1])

    h = _ffn1(x.reshape(b * t, d), row(ffn1_norm), bf(ffn1_w_gate), bf(ffn1_w_up), bf(ffn1_w_down))
    qat, ka, vat, qbt, kb, vbt, ga, gb = _inproj(h, row(mix_norm), bf(w_in), cos_t, sin_t, t)
    ya = _natten(qat, seq3(ka), vat, _na_bias_tables(na_rpb[0].astype(_F32) * LOG2E, rows))
    yb = _swa(qbt, seq3(kb), vbt, sink_logit[0].astype(_F32))
    y = _merge(h, ya.reshape(b * t, -1), yb.reshape(b * t, -1), ga, gb,
               bf(w_branch_a), bf(w_branch_b), bf(w_out),
               row(ffn2_norm), bf(ffn2_w_gate), bf(ffn2_w_up), bf(ffn2_w_down), row(final_norm))
    return y.reshape(b, t, d)
```

```python
import functools

import jax
import jax.numpy as jnp
import numpy as np
from jax import lax
from jax.experimental import pallas as pl
from jax.experimental.pallas import tpu as pltpu

D_MODEL = 1024
HEAD_DIM = 64
NA_HEADS = 8
NB_HEADS = 8
NB_KV_HEADS = 2
GRID_W = 64
NA_KH = 8
NA_KW = 16
WIN = 128
ROPE_THETA = 10000.0
D_FF = 2816
EPS = 1e-6

WIDTH_A = NA_HEADS * HEAD_DIM
WIDTH_BQ = NB_HEADS * HEAD_DIM
WIDTH_BKV = NB_KV_HEADS * HEAD_DIM
LOG2E = 1.4426950408889634
Q_SCALE = HEAD_DIM ** -0.5 * LOG2E
MASK_VALUE = -1e30
LANES = 128
BF16_SUBLANES = 16

FFN1_BLOCK = 1024
MERGE_BLOCK = 512
DENSE_SUB_TILE = 256
FF_CHUNKS = ((0, 1536), (1536, 2816))
NA_Q_ROWS = 4
NA_K_ROWS = NA_Q_ROWS + NA_KH
SWA_Q = 256
SWA_K = SWA_Q + 2 * WIN
ATTN_SUBSTEPS = 4
NA_SCORES_AHEAD = 4
SWA_SCORES_AHEAD = 6
VMEM_LIMIT = 56 * 1024 * 1024

_BF16 = jnp.bfloat16
_F32 = jnp.float32


def _rms(x, g):
    return x * lax.rsqrt(jnp.mean(x * x, axis=-1, keepdims=True) + EPS) * g


def _sigmoid(x):
    return 1.0 / (1.0 + jnp.exp(-x))


def _swiglu(xb, wg_ref, wu_ref, wd_ref, after_first_dots=None):
    acc = None
    for c, (lo, hi) in enumerate(FF_CHUNKS):
        g = jnp.dot(xb, wg_ref[:, lo:hi], preferred_element_type=_F32)
        u = jnp.dot(xb, wu_ref[:, lo:hi], preferred_element_type=_F32)
        if c == 0 and after_first_dots is not None:
            after_first_dots()
        a = (g * _sigmoid(g) * u).astype(_BF16)
        part = jnp.dot(a, wd_ref[lo:hi, :], preferred_element_type=_F32)
        acc = part if acc is None else acc + part
    return acc


def _const_spec(shape):
    return pl.BlockSpec(shape, lambda *_: (0,) * len(shape), pipeline_mode=pl.Buffered(1))


def _sub_tiles(n_rows):
    return [slice(lo, lo + DENSE_SUB_TILE) for lo in range(0, n_rows, DENSE_SUB_TILE)]


def _for_each_substep(count, size, body):
    def step(i, carry):
        body(pl.multiple_of(i * size, size))
        return carry
    lax.fori_loop(0, count, step, 0)


def _ffn1_kernel(x_ref, g_ref, wg_ref, wu_ref, wd_ref, h_ref):
    rows = _sub_tiles(x_ref.shape[0])
    normed = {0: _rms(x_ref[rows[0], :], g_ref[...]).astype(_BF16)}

    for i in range(len(rows)):
        def norm_next(i=i):
            if i + 1 < len(rows):
                normed[i + 1] = _rms(x_ref[rows[i + 1], :], g_ref[...]).astype(_BF16)

        f = _swiglu(normed.pop(i), wg_ref, wu_ref, wd_ref, after_first_dots=norm_next)
        h_ref[rows[i], :] = x_ref[rows[i], :] + 0.5 * f


def _ffn1(x2d, gain, wg, wu, wd):
    n, d = x2d.shape
    block = FFN1_BLOCK
    tile = pl.BlockSpec((block, d), lambda i: (i, 0))
    return pl.pallas_call(
        _ffn1_kernel,
        grid=(n // block,),
        in_specs=[tile, _const_spec((1, d)), _const_spec(wg.shape), _const_spec(wu.shape), _const_spec(wd.shape)],
        out_specs=tile,
        out_shape=jax.ShapeDtypeStruct((n, d), _F32),
        compiler_params=pltpu.CompilerParams(dimension_semantics=("parallel",), vmem_limit_bytes=VMEM_LIMIT),
        name="ffn1",
    )(x2d, gain, wg, wu, wd)


_OFF_QA, _OFF_KA, _OFF_VA = 0, WIDTH_A, 2 * WIDTH_A
_OFF_QB = 3 * WIDTH_A
_OFF_KVB = _OFF_QB + WIDTH_BQ
_OFF_GA = _OFF_KVB + 2 * WIDTH_BKV
_OFF_GB = _OFF_GA + D_MODEL
D_IN = _OFF_GB + D_MODEL


def _rope(x, cos, sin_signed, first_half):
    w = x.shape[-1]
    half = HEAD_DIM // 2
    partner = jnp.where(first_half, pltpu.roll(x, w - half, 1), pltpu.roll(x, half, 1))
    return x * cos + partner * sin_signed


def _inproj_kernel(h_ref, g_ref, w_ref, cos_ref, sin_ref,
                   qat_ref, ka_ref, vat_ref, qbt_ref, kb_ref, vbt_ref, ga_ref, gb_ref):
    all_rows = _sub_tiles(h_ref.shape[0])
    n_sub = len(all_rows)
    normed = {0: _rms(h_ref[all_rows[0], :], g_ref[...]).astype(_BF16)}

    for i in range(n_sub):
        rows = all_rows[i]
        u = normed.pop(i)

        def proj(lo, hi):
            return jnp.dot(u, w_ref[:, lo:hi], preferred_element_type=_F32)

        qa = proj(_OFF_QA, _OFF_KA)
        if i + 1 < n_sub:
            normed[i + 1] = _rms(h_ref[all_rows[i + 1], :], g_ref[...]).astype(_BF16)
        qat_ref[:, rows] = (qa * Q_SCALE).T.astype(_BF16)
        ka_ref[rows, :] = proj(_OFF_KA, _OFF_VA).astype(_BF16)
        vat_ref[:, rows] = proj(_OFF_VA, _OFF_QB).T.astype(_BF16)

        cos = cos_ref[rows, :]
        sin = sin_ref[rows, :]
        lane = lax.broadcasted_iota(jnp.int32, cos.shape, 1)
        first = (lane % HEAD_DIM) < (HEAD_DIM // 2)
        reps = WIDTH_BQ // cos.shape[-1]
        cos_q = jnp.concatenate([cos] * reps, axis=-1)
        sin_q = jnp.concatenate([sin] * reps, axis=-1)
        first_q = jnp.concatenate([first] * reps, axis=-1)
        qb = _rope(proj(_OFF_QB, _OFF_KVB), cos_q, sin_q, first_q)
        qbt_ref[:, rows] = (qb * Q_SCALE).T.astype(_BF16)

        kv = proj(_OFF_KVB, _OFF_GA)
        kb_ref[rows, :] = _rope(kv[:, :WIDTH_BKV], cos, sin, first).astype(_BF16)
        vbt_ref[:, rows] = kv[:, WIDTH_BKV:].T.astype(_BF16)

        ga_ref[rows, :] = _sigmoid(proj(_OFF_GA, _OFF_GB)).astype(_BF16)
        gb_ref[rows, :] = _sigmoid(proj(_OFF_GB, D_IN)).astype(_BF16)


def _inproj(h2d, gain, w_in, cos_t, sin_t, seq):
    n, d = h2d.shape
    block = FFN1_BLOCK
    blocks_per_seq = seq // block

    def tile(width):
        return pl.BlockSpec((block, width), lambda i: (i, 0))

    def tposed(width):
        return pl.BlockSpec((None, width, block), lambda i: (i // blocks_per_seq, 0, i % blocks_per_seq))

    table = pl.BlockSpec((block, cos_t.shape[-1]), lambda i: (i % blocks_per_seq, 0))
    outs = [
        (True, WIDTH_A), (False, WIDTH_A), (True, WIDTH_A), (True, WIDTH_BQ), (False, WIDTH_BKV), (True, WIDTH_BKV),
        (False, D_MODEL), (False, D_MODEL)]
    return pl.pallas_call(
        _inproj_kernel,
        grid=(n // block,),
        in_specs=[tile(d), _const_spec((1, d)), _const_spec(w_in.shape), table, table],
        out_specs=[tposed(w) if t else tile(w) for t, w in outs],
        out_shape=[jax.ShapeDtypeStruct((n // seq, w, seq) if t else (n, w), _BF16) for t, w in outs],
        compiler_params=pltpu.CompilerParams(dimension_semantics=("parallel",), vmem_limit_bytes=VMEM_LIMIT),
        name="inproj",
    )(h2d, gain, w_in, cos_t, sin_t)


def _rope_tables(seq):
    half = HEAD_DIM // 2
    inv = ROPE_THETA ** (-jnp.arange(half, dtype=_F32) / half)
    ang = jnp.arange(seq).astype(_F32)[:, None] * inv[None, :]
    cos, sin = jnp.cos(ang), jnp.sin(ang)
    cos_h = jnp.concatenate([cos, cos], axis=-1)
    sin_h = jnp.concatenate([-sin, sin], axis=-1)
    return jnp.tile(cos_h, (1, 2)), jnp.tile(sin_h, (1, 2))


def _softmax_pv(s, vt, key_ranges, sink=None):
    nk, nq = s.shape
    probs, maxes = [], []
    for t, (lo, hi) in enumerate(key_ranges):
        band = s[lo:hi, t * LANES:(t + 1) * LANES]
        m = jnp.max(band, axis=0, keepdims=True)
        if sink is not None:
            m = jnp.maximum(m, sink)
        e = jnp.exp2(band - m).astype(_BF16)
        pads = [jnp.zeros((n, LANES), _BF16) for n in (lo, nk - hi)]
        probs.append(jnp.concatenate([x for x in (pads[0], e, pads[1]) if x.shape[0]], axis=0))
        maxes.append(m)
    p = jnp.concatenate(probs, axis=1)
    vt_ones = jnp.concatenate([vt, jnp.ones((BF16_SUBLANES, nk), _BF16)], axis=0)
    ov = jnp.dot(vt_ones, p, preferred_element_type=_F32)
    l = ov[HEAD_DIM:HEAD_DIM + 1, :]
    if sink is not None:
        l = l + jnp.exp2(sink - jnp.concatenate(maxes, axis=1))
    return ov[:HEAD_DIM, :] * (1.0 / l)


def _attend_heads(n_heads, ahead, scores, values, key_ranges, store_pair, sink=None):
    pending = [scores(h) for h in range(ahead)]
    outs = []
    for h in range(n_heads):
        if h + ahead < n_heads:
            pending.append(scores(h + ahead))
        outs.append(_softmax_pv(pending.pop(0), values(h), key_ranges, None if sink is None else sink(h)))
        if h % 2:
            store_pair(h - 1, jnp.concatenate(outs[-2:], axis=0))


def _na_bias_tables(rpb, rows):
    heads, n_dr, n_dc = rpb.shape
    span = 2 * GRID_W
    ring = jnp.concatenate(
        [rpb[..., NA_KW - 1:], jnp.zeros((heads, n_dr, span - n_dc), rpb.dtype), rpb[..., :NA_KW - 1]], axis=-1)
    skew = jnp.tile(ring, (1, 1, GRID_W))[..., :GRID_W * (span - 1)].reshape(heads, n_dr, GRID_W, span - 1)
    c = np.arange(GRID_W)
    cs = np.clip(c - NA_KW // 2, 0, GRID_W - NA_KW)
    col_ok = (c[None, :] >= cs[:, None]) & (c[None, :] < cs[:, None] + NA_KW)
    toe = jnp.where(col_ok, skew[..., :GRID_W], MASK_VALUE)
    toe = jnp.swapaxes(toe, -1, -2)
    masked = jnp.full((heads, GRID_W, GRID_W), MASK_VALUE, rpb.dtype)

    groups = rows // NA_Q_ROWS
    tables = []
    for group in (0, 1, groups - 1):
        key_row0 = int(np.clip(group * NA_Q_ROWS - NA_KH // 2, 0, rows - NA_K_ROWS))
        key_blocks = []
        for kr in key_row0 + np.arange(NA_K_ROWS):
            query_blocks = []
            for qr in group * NA_Q_ROWS + np.arange(NA_Q_ROWS):
                rs = int(np.clip(qr - NA_KH // 2, 0, rows - NA_KH))
                inside = rs <= kr < rs + NA_KH
                query_blocks.append(toe[:, kr - qr + NA_KH - 1] if inside else masked)
            key_blocks.append(jnp.concatenate(query_blocks, axis=-1))
        tables.append(jnp.concatenate(key_blocks, axis=1))
    return jnp.stack(tables)


def _natten_kernel(qt_ref, k_ref, vt_ref, bias_mid_ref, bias_edge_ref, o_ref, *, groups):
    blk = NA_Q_ROWS * GRID_W
    nk = NA_K_ROWS * GRID_W
    pair = 2 * HEAD_DIM
    upper = lax.broadcasted_iota(jnp.int32, (pair, blk), 0) >= HEAD_DIM

    def substep(qoff):
        a = pl.program_id(1) * ATTN_SUBSTEPS + qoff // blk
        start = pl.multiple_of(jnp.clip(a - 1, 0, groups - 3) * blk, blk)

        def run(key_ranges, bias_ref):
            def scores(h):
                rows = slice(h // 2 * pair, (h // 2 + 1) * pair)
                k2 = k_ref[pl.ds(start, nk), rows]
                q2 = qt_ref[rows, pl.ds(qoff, blk)]
                qh = jnp.where(upper if h % 2 else ~upper, q2, jnp.zeros_like(q2))
                return jnp.dot(k2, qh, preferred_element_type=_F32) + bias_ref[h]

            def values(h):
                return vt_ref[h * HEAD_DIM:(h + 1) * HEAD_DIM, pl.ds(start, nk)]

            def store_pair(h, o2):
                o_ref[pl.ds(qoff, blk), h * HEAD_DIM:(h + 2) * HEAD_DIM] = o2.T.astype(_BF16)

            _attend_heads(NA_HEADS, NA_SCORES_AHEAD, scores, values, key_ranges, store_pair)

        rows_per_tile = LANES // GRID_W
        mid_ranges = tuple((t * rows_per_tile * GRID_W, (t * rows_per_tile + rows_per_tile + NA_KH - 1) * GRID_W)
                           for t in range(blk // LANES))
        mid = jnp.logical_and(a >= 1, a <= groups - 2)
        pl.when(mid)(lambda: run(mid_ranges, bias_mid_ref))
        pl.when(jnp.logical_not(mid))(lambda: run(((0, nk),) * (blk // LANES), bias_edge_ref))

    _for_each_substep(ATTN_SUBSTEPS, blk, substep)


def _natten(qt, k, vt, bias):
    b, t, w = k.shape
    blk = NA_Q_ROWS * GRID_W
    groups = t // blk
    steps = groups // ATTN_SUBSTEPS
    assert NA_K_ROWS * GRID_W == 3 * blk and groups >= 3 and steps * ATTN_SUBSTEPS == groups
    bias_mid, bias_edge = bias[1], bias[0::2]
    return pl.pallas_call(
        functools.partial(_natten_kernel, groups=groups),
        grid=(b, steps),
        in_specs=[pl.BlockSpec((None, w, ATTN_SUBSTEPS * blk), lambda bi, s: (bi, 0, s)),
                  pl.BlockSpec((None, t, w), lambda bi, s: (bi, 0, 0)),
                  pl.BlockSpec((None, w, t), lambda bi, s: (bi, 0, 0)),
                  _const_spec(bias_mid.shape),
                  pl.BlockSpec((None,) + bias_edge.shape[1:], lambda bi, s: (s // (steps - 1), 0, 0, 0),
                               pipeline_mode=pl.Buffered(1))],
        out_specs=pl.BlockSpec((None, ATTN_SUBSTEPS * blk, w), lambda bi, s: (bi, s, 0)),
        out_shape=jax.ShapeDtypeStruct((b, t, w), _BF16),
        compiler_params=pltpu.CompilerParams(
            dimension_semantics=("parallel", "arbitrary"), vmem_limit_bytes=VMEM_LIMIT),
        name="natten",
    )(qt, k, vt, bias_mid, bias_edge)


def _swa_mask(key_pos0, query_pos0, xp=jnp):
    iota = lax.broadcasted_iota if xp is jnp else (lambda dt, shape, dim: np.indices(shape)[dim])
    kpos = key_pos0 + iota(jnp.int32, (SWA_K, SWA_Q), 0)
    qpos = query_pos0 + iota(jnp.int32, (SWA_K, SWA_Q), 1)
    return xp.where(xp.abs(qpos - kpos) <= WIN, 0.0, MASK_VALUE).astype(xp.float32)


def _swa_kernel(sink_ref, qt_ref, k_ref, vt_ref, mask_mid_ref, o_ref, *, seq):
    group = NB_HEADS // NB_KV_HEADS
    zeros = jnp.zeros((HEAD_DIM, SWA_Q), _BF16)

    def substep(qoff):
        q0 = pl.program_id(1) * (ATTN_SUBSTEPS * SWA_Q) + qoff
        start = pl.multiple_of(jnp.clip(q0 - WIN, 0, seq - SWA_K), WIN)

        def run(key_ranges, mask):
            k2 = k_ref[pl.ds(start, SWA_K), :]

            def scores(h):
                qh = qt_ref[h * HEAD_DIM:(h + 1) * HEAD_DIM, pl.ds(qoff, SWA_Q)]
                qh = jnp.concatenate([qh, zeros] if h // group == 0 else [zeros, qh], axis=0)
                return jnp.dot(k2, qh, preferred_element_type=_F32) + mask

            def values(h):
                j = h // group
                return vt_ref[j * HEAD_DIM:(j + 1) * HEAD_DIM, pl.ds(start, SWA_K)]

            def store_pair(h, o2):
                o_ref[pl.ds(qoff, SWA_Q), h * HEAD_DIM:(h + 2) * HEAD_DIM] = o2.T.astype(_BF16)

            _attend_heads(NB_HEADS, SWA_SCORES_AHEAD, scores, values, key_ranges, store_pair,
                          sink=lambda h: sink_ref[h] * LOG2E)

        n_tiles = SWA_Q // LANES
        mid_ranges = tuple((t * LANES, t * LANES + LANES + 2 * WIN) for t in range(n_tiles))
        mid = jnp.logical_and(q0 - WIN >= 0, q0 - WIN <= seq - SWA_K)
        pl.when(mid)(lambda: run(mid_ranges, mask_mid_ref[...]))
        pl.when(jnp.logical_not(mid))(lambda: run(((0, SWA_K),) * n_tiles, _swa_mask(start, q0)))

    _for_each_substep(ATTN_SUBSTEPS, SWA_Q, substep)


def _swa(qt, k, vt, sink):
    b, w, t = qt.shape
    assert k.shape[-1] == NB_KV_HEADS * HEAD_DIM == 2 * HEAD_DIM
    block = ATTN_SUBSTEPS * SWA_Q
    mask_mid = jnp.asarray(_swa_mask(0, WIN, xp=np))
    return pl.pallas_call(
        functools.partial(_swa_kernel, seq=t),
        grid=(b, t // block),
        in_specs=[pl.BlockSpec(memory_space=pltpu.SMEM),
                  pl.BlockSpec((None, w, block), lambda bi, n: (bi, 0, n)),
                  pl.BlockSpec((None, t, k.shape[-1]), lambda bi, n: (bi, 0, 0)),
                  pl.BlockSpec((None, vt.shape[1], t), lambda bi, n: (bi, 0, 0)),
                  _const_spec(mask_mid.shape)],
        out_specs=pl.BlockSpec((None, block, w), lambda bi, n: (bi, n, 0)),
        out_shape=jax.ShapeDtypeStruct((b, t, w), _BF16),
        compiler_params=pltpu.CompilerParams(
            dimension_semantics=("parallel", "arbitrary"), vmem_limit_bytes=VMEM_LIMIT),
        name="swa",
    )(sink, qt, k, vt, mask_mid)


def _merge_kernel(h_ref, ya_ref, yb_ref, ga_ref, gb_ref, wa_ref, wb_ref, wo_ref,
                  g2_ref, wg_ref, wu_ref, wd_ref, gf_ref, y_ref):
    rows = _sub_tiles(h_ref.shape[0])
    n_sub = len(rows)
    hs = []
    for r in rows:
        ma = jnp.dot(ya_ref[r, :], wa_ref[...], preferred_element_type=_F32)
        mb = jnp.dot(yb_ref[r, :], wb_ref[...], preferred_element_type=_F32)
        merged = ga_ref[r, :].astype(_F32) * ma + gb_ref[r, :].astype(_F32) * mb
        hs.append(h_ref[r, :] + jnp.dot(merged.astype(_BF16), wo_ref[...], preferred_element_type=_F32))

    normed = {0: _rms(hs[0], g2_ref[...]).astype(_BF16)}
    for i in range(n_sub):
        def norm_next(i=i):
            if i + 1 < n_sub:
                normed[i + 1] = _rms(hs[i + 1], g2_ref[...]).astype(_BF16)

        h = hs[i] + 0.5 * _swiglu(normed.pop(i), wg_ref, wu_ref, wd_ref, after_first_dots=norm_next)
        y_ref[rows[i], :] = _rms(h, gf_ref[...])


def _merge(h2d, ya, yb, ga, gb, wa, wb, wo, g2, wg, wu, wd, gf):
    n, d = h2d.shape

    def tile(width):
        return pl.BlockSpec((MERGE_BLOCK, width), lambda i: (i, 0))

    consts = (wa, wb, wo, g2, wg, wu, wd, gf)
    return pl.pallas_call(
        _merge_kernel,
        grid=(n // MERGE_BLOCK,),
        in_specs=[tile(d), tile(ya.shape[-1]), tile(yb.shape[-1]), tile(d), tile(d)]
                 + [_const_spec(c.shape) for c in consts],
        out_specs=tile(d),
        out_shape=jax.ShapeDtypeStruct((n, d), _F32),
        compiler_params=pltpu.CompilerParams(dimension_semantics=("parallel",), vmem_limit_bytes=VMEM_LIMIT),
        name="merge_ffn2",
    )(h2d, ya, yb, ga, gb, *consts)


def kernel(x, ffn1_norm, ffn1_w_gate, ffn1_w_up, ffn1_w_down, mix_norm, w_in, na_rpb, sink_logit, w_branch_a, w_branch_b, w_out, ffn2_norm, ffn2_w_gate, ffn2_w_up, ffn2_w_down, final_norm):
    b, t, d = x.shape
    assert ffn1_norm.shape[0] == 1, "single-layer trunk: the final norm is fused into the layer's last kernel"
    rows = t // GRID_W
    cos_t, sin_t = _rope_tables(t)
    bf = lambda w: w[0].astype(_BF16)
    row = lambda g: g.reshape(1, d).astype(_F32)
    seq3 = lambda a: a.reshape(b, t, a.shape[-1])

    h = _ffn1(x.reshape(b * t, d), row(ffn1_norm), bf(ffn1_w_gate), bf(ffn1_w_up), bf(ffn1_w_down))
    qat, ka, vat, qbt, kb, vbt, ga, gb = _inproj(h, row(mix_norm), bf(w_in), cos_t, sin_t, t)
    ya = _natten(qat, seq3(ka), vat, _na_bias_tables(na_rpb[0].astype(_F32) * LOG2E, rows))
    yb = _swa(qbt, seq3(kb), vbt, sink_logit[0].astype(_F32))
    y = _merge(h, ya.reshape(b * t, -1), yb.reshape(b * t, -1), ga, gb,
               bf(w_branch_a), bf(w_branch_b), bf(w_out),
               row(ffn2_norm), bf(ffn2_w_gate), bf(ffn2_w_up), bf(ffn2_w_down), row(final_norm))
    return y.reshape(b, t, d)
```

```python
import functools

import jax
import jax.numpy as jnp
import numpy as np
from jax import lax
from jax.experimental import pallas as pl
from jax.experimental.pallas import tpu as pltpu

D_MODEL = 1024
HEAD_DIM = 64
NA_HEADS = 8
NB_HEADS = 8
NB_KV_HEADS = 2
GRID_W = 64
NA_KH = 8
NA_KW = 16
WIN = 128
ROPE_THETA = 10000.0
D_FF = 2816
EPS = 1e-6

WIDTH_A = NA_HEADS * HEAD_DIM
WIDTH_BQ = NB_HEADS * HEAD_DIM
WIDTH_BKV = NB_KV_HEADS * HEAD_DIM
LOG2E = 1.4426950408889634
Q_SCALE = HEAD_DIM ** -0.5 * LOG2E
MASK_VALUE = -1e30
LANES = 128
BF16_SUBLANES = 16

FFN1_BLOCK = 1024
MERGE_BLOCK = 1024
DENSE_SUB_TILE = 256
FF_CHUNKS = ((0, 1536), (1536, 2816))
NA_Q_ROWS = 4
NA_K_ROWS = NA_Q_ROWS + NA_KH
SWA_Q = 256
SWA_K = SWA_Q + 2 * WIN
ATTN_SUBSTEPS = 4
NA_SCORES_AHEAD = 4
SWA_SCORES_AHEAD = 6
VMEM_LIMIT = 60 * 1024 * 1024

_BF16 = jnp.bfloat16
_F32 = jnp.float32


def _rms(x, g):
    return x * lax.rsqrt(jnp.mean(x * x, axis=-1, keepdims=True) + EPS) * g


def _sigmoid(x):
    return 1.0 / (1.0 + jnp.exp(-x))


def _swiglu(xb, wg_ref, wu_ref, wd_ref, after_first_dots=None):
    acc = None
    for c, (lo, hi) in enumerate(FF_CHUNKS):
        g = jnp.dot(xb, wg_ref[:, lo:hi], preferred_element_type=_F32)
        u = jnp.dot(xb, wu_ref[:, lo:hi], preferred_element_type=_F32)
        if c == 0 and after_first_dots is not None:
            after_first_dots()
        a = (g * _sigmoid(g) * u).astype(_BF16)
        part = jnp.dot(a, wd_ref[lo:hi, :], preferred_element_type=_F32)
        acc = part if acc is None else acc + part
    return acc


def _const_spec(shape):
    return pl.BlockSpec(shape, lambda *_: (0,) * len(shape), pipeline_mode=pl.Buffered(1))


def _sub_tiles(n_rows):
    return [slice(lo, lo + DENSE_SUB_TILE) for lo in range(0, n_rows, DENSE_SUB_TILE)]


def _for_each_substep(count, size, body):
    def step(i, carry):
        body(pl.multiple_of(i * size, size))
        return carry
    lax.fori_loop(0, count, step, 0)


def _ffn1_kernel(x_ref, g_ref, wg_ref, wu_ref, wd_ref, h_ref):
    rows = _sub_tiles(x_ref.shape[0])
    normed = {0: _rms(x_ref[rows[0], :], g_ref[...]).astype(_BF16)}

    for i in range(len(rows)):
        def norm_next(i=i):
            if i + 1 < len(rows):
                normed[i + 1] = _rms(x_ref[rows[i + 1], :], g_ref[...]).astype(_BF16)

        f = _swiglu(normed.pop(i), wg_ref, wu_ref, wd_ref, after_first_dots=norm_next)
        h_ref[rows[i], :] = x_ref[rows[i], :] + 0.5 * f


def _ffn1(x2d, gain, wg, wu, wd):
    n, d = x2d.shape
    block = FFN1_BLOCK
    tile = pl.BlockSpec((block, d), lambda i: (i, 0))
    return pl.pallas_call(
        _ffn1_kernel,
        grid=(n // block,),
        in_specs=[tile, _const_spec((1, d)), _const_spec(wg.shape), _const_spec(wu.shape), _const_spec(wd.shape)],
        out_specs=tile,
        out_shape=jax.ShapeDtypeStruct((n, d), _F32),
        compiler_params=pltpu.CompilerParams(dimension_semantics=("parallel",), vmem_limit_bytes=VMEM_LIMIT),
        name="ffn1",
    )(x2d, gain, wg, wu, wd)


_OFF_QA, _OFF_KA, _OFF_VA = 0, WIDTH_A, 2 * WIDTH_A
_OFF_QB = 3 * WIDTH_A
_OFF_KVB = _OFF_QB + WIDTH_BQ
_OFF_GA = _OFF_KVB + 2 * WIDTH_BKV
_OFF_GB = _OFF_GA + D_MODEL
D_IN = _OFF_GB + D_MODEL


def _rope(x, cos, sin_signed, first_half):
    w = x.shape[-1]
    half = HEAD_DIM // 2
    partner = jnp.where(first_half, pltpu.roll(x, w - half, 1), pltpu.roll(x, half, 1))
    return x * cos + partner * sin_signed


def _inproj_kernel(h_ref, g_ref, w_ref, cos_ref, sin_ref,
                   qat_ref, ka_ref, vat_ref, qbt_ref, kb_ref, vbt_ref, ga_ref, gb_ref):
    all_rows = _sub_tiles(h_ref.shape[0])
    n_sub = len(all_rows)
    normed = {0: _rms(h_ref[all_rows[0], :], g_ref[...]).astype(_BF16)}

    for i in range(n_sub):
        rows = all_rows[i]
        u = normed.pop(i)

        def proj(lo, hi):
            return jnp.dot(u, w_ref[:, lo:hi], preferred_element_type=_F32)

        qa = proj(_OFF_QA, _OFF_KA)
        if i + 1 < n_sub:
            normed[i + 1] = _rms(h_ref[all_rows[i + 1], :], g_ref[...]).astype(_BF16)
        qat_ref[:, rows] = (qa * Q_SCALE).T.astype(_BF16)
        ka_ref[rows, :] = proj(_OFF_KA, _OFF_VA).astype(_BF16)
        vat_ref[:, rows] = proj(_OFF_VA, _OFF_QB).T.astype(_BF16)

        cos = cos_ref[rows, :]
        sin = sin_ref[rows, :]
        lane = lax.broadcasted_iota(jnp.int32, cos.shape, 1)
        first = (lane % HEAD_DIM) < (HEAD_DIM // 2)
        reps = WIDTH_BQ // cos.shape[-1]
        cos_q = jnp.concatenate([cos] * reps, axis=-1)
        sin_q = jnp.concatenate([sin] * reps, axis=-1)
        first_q = jnp.concatenate([first] * reps, axis=-1)
        qb = _rope(proj(_OFF_QB, _OFF_KVB), cos_q, sin_q, first_q)
        qbt_ref[:, rows] = (qb * Q_SCALE).T.astype(_BF16)

        kv = proj(_OFF_KVB, _OFF_GA)
        kb_ref[rows, :] = _rope(kv[:, :WIDTH_BKV], cos, sin, first).astype(_BF16)
        vbt_ref[:, rows] = kv[:, WIDTH_BKV:].T.astype(_BF16)

        ga_ref[rows, :] = _sigmoid(proj(_OFF_GA, _OFF_GB)).astype(_BF16)
        gb_ref[rows, :] = _sigmoid(proj(_OFF_GB, D_IN)).astype(_BF16)


def _inproj(h2d, gain, w_in, cos_t, sin_t, seq):
    n, d = h2d.shape
    block = FFN1_BLOCK
    blocks_per_seq = seq // block

    def tile(width):
        return pl.BlockSpec((block, width), lambda i: (i, 0))

    def tposed(width):
        return pl.BlockSpec((None, width, block), lambda i: (i // blocks_per_seq, 0, i % blocks_per_seq))

    table = pl.BlockSpec((block, cos_t.shape[-1]), lambda i: (i % blocks_per_seq, 0))
    outs = [
        (True, WIDTH_A), (False, WIDTH_A), (True, WIDTH_A), (True, WIDTH_BQ), (False, WIDTH_BKV), (True, WIDTH_BKV),
        (False, D_MODEL), (False, D_MODEL)]
    return pl.pallas_call(
        _inproj_kernel,
        grid=(n // block,),
        in_specs=[tile(d), _const_spec((1, d)), _const_spec(w_in.shape), table, table],
        out_specs=[tposed(w) if t else tile(w) for t, w in outs],
        out_shape=[jax.ShapeDtypeStruct((n // seq, w, seq) if t else (n, w), _BF16) for t, w in outs],
        compiler_params=pltpu.CompilerParams(dimension_semantics=("parallel",), vmem_limit_bytes=VMEM_LIMIT),
        name="inproj",
    )(h2d, gain, w_in, cos_t, sin_t)


def _rope_tables(seq):
    half = HEAD_DIM // 2
    inv = ROPE_THETA ** (-jnp.arange(half, dtype=_F32) / half)
    ang = jnp.arange(seq).astype(_F32)[:, None] * inv[None, :]
    cos, sin = jnp.cos(ang), jnp.sin(ang)
    cos_h = jnp.concatenate([cos, cos], axis=-1)
    sin_h = jnp.concatenate([-sin, sin], axis=-1)
    return jnp.tile(cos_h, (1, 2)), jnp.tile(sin_h, (1, 2))


def _softmax_pv(s, vt, key_ranges, sink=None):
    nk, nq = s.shape
    probs, maxes = [], []
    for t, (lo, hi) in enumerate(key_ranges):
        band = s[lo:hi, t * LANES:(t + 1) * LANES]
        m = jnp.max(band, axis=0, keepdims=True)
        if sink is not None:
            m = jnp.maximum(m, sink)
        e = jnp.exp2(band - m).astype(_BF16)
        pads = [jnp.zeros((n, LANES), _BF16) for n in (lo, nk - hi)]
        probs.append(jnp.concatenate([x for x in (pads[0], e, pads[1]) if x.shape[0]], axis=0))
        maxes.append(m)
    p = jnp.concatenate(probs, axis=1)
    vt_ones = jnp.concatenate([vt, jnp.ones((BF16_SUBLANES, nk), _BF16)], axis=0)
    ov = jnp.dot(vt_ones, p, preferred_element_type=_F32)
    l = ov[HEAD_DIM:HEAD_DIM + 1, :]
    if sink is not None:
        l = l + jnp.exp2(sink - jnp.concatenate(maxes, axis=1))
    return ov[:HEAD_DIM, :] * (1.0 / l)


def _attend_heads(n_heads, ahead, scores, values, key_ranges, store_pair, sink=None):
    pending = [scores(h) for h in range(ahead)]
    outs = []
    for h in range(n_heads):
        if h + ahead < n_heads:
            pending.append(scores(h + ahead))
        outs.append(_softmax_pv(pending.pop(0), values(h), key_ranges, None if sink is None else sink(h)))
        if h % 2:
            store_pair(h - 1, jnp.concatenate(outs[-2:], axis=0))


def _na_bias_tables(rpb, rows):
    heads, n_dr, n_dc = rpb.shape
    span = 2 * GRID_W
    ring = jnp.concatenate(
        [rpb[..., NA_KW - 1:], jnp.zeros((heads, n_dr, span - n_dc), rpb.dtype), rpb[..., :NA_KW - 1]], axis=-1)
    skew = jnp.tile(ring, (1, 1, GRID_W))[..., :GRID_W * (span - 1)].reshape(heads, n_dr, GRID_W, span - 1)
    c = np.arange(GRID_W)
    cs = np.clip(c - NA_KW // 2, 0, GRID_W - NA_KW)
    col_ok = (c[None, :] >= cs[:, None]) & (c[None, :] < cs[:, None] + NA_KW)
    toe = jnp.where(col_ok, skew[..., :GRID_W], MASK_VALUE)
    toe = jnp.swapaxes(toe, -1, -2)
    masked = jnp.full((heads, GRID_W, GRID_W), MASK_VALUE, rpb.dtype)

    def tile(query_row0, key_row0, n_key_rows):
        key_blocks = []
        for kr in key_row0 + np.arange(n_key_rows):
            query_blocks = []
            for qr in query_row0 + np.arange(NA_Q_ROWS):
                rs = int(np.clip(qr - NA_KH // 2, 0, rows - NA_KH))
                inside = rs <= kr < rs + NA_KH
                query_blocks.append(toe[:, kr - qr + NA_KH - 1] if inside else masked)
            key_blocks.append(jnp.concatenate(query_blocks, axis=-1))
        return jnp.concatenate(key_blocks, axis=1)

    mid = tile(NA_Q_ROWS, NA_Q_ROWS - NA_KH // 2, NA_K_ROWS)
    edge = jnp.stack([tile(0, 0, NA_KH), tile(rows - NA_Q_ROWS, rows - NA_KH, NA_KH)])
    return mid, edge


def _natten_kernel(qt_ref, k_ref, vt_ref, bias_mid_ref, bias_edge_ref, o_ref, *, groups):
    blk = NA_Q_ROWS * GRID_W
    n_tiles = blk // LANES
    pair = 2 * HEAD_DIM
    upper = lax.broadcasted_iota(jnp.int32, (pair, blk), 0) >= HEAD_DIM

    def substep(qoff):
        a = pl.program_id(1) * ATTN_SUBSTEPS + qoff // blk

        def run(start, nk, key_ranges, bias_ref):
            def scores(h):
                rows = slice(h // 2 * pair, (h // 2 + 1) * pair)
                k2 = k_ref[pl.ds(start, nk), rows]
                q2 = qt_ref[rows, pl.ds(qoff, blk)]
                qh = jnp.where(upper if h % 2 else ~upper, q2, jnp.zeros_like(q2))
                return jnp.dot(k2, qh, preferred_element_type=_F32) + bias_ref[h]

            def values(h):
                return vt_ref[h * HEAD_DIM:(h + 1) * HEAD_DIM, pl.ds(start, nk)]

            def store_pair(h, o2):
                o_ref[pl.ds(qoff, blk), h * HEAD_DIM:(h + 2) * HEAD_DIM] = o2.T.astype(_BF16)

            _attend_heads(NA_HEADS, NA_SCORES_AHEAD, scores, values, key_ranges, store_pair)

        rows_per_tile = LANES // GRID_W
        mid_ranges = tuple((t * rows_per_tile * GRID_W, (t * rows_per_tile + rows_per_tile + NA_KH - 1) * GRID_W)
                           for t in range(n_tiles))
        mid_start = pl.multiple_of((a * NA_Q_ROWS - NA_KH // 2) * GRID_W, blk)
        nk_edge = NA_KH * GRID_W
        edge_start = pl.multiple_of(jnp.where(a == 0, 0, groups * blk - nk_edge), blk)
        mid = jnp.logical_and(a >= 1, a <= groups - 2)
        pl.when(mid)(lambda: run(mid_start, NA_K_ROWS * GRID_W, mid_ranges, bias_mid_ref))
        pl.when(jnp.logical_not(mid))(lambda: run(edge_start, nk_edge, ((0, nk_edge),) * n_tiles, bias_edge_ref))

    _for_each_substep(ATTN_SUBSTEPS, blk, substep)


def _natten(qt, k, vt, bias_mid, bias_edge):
    b, t, w = k.shape
    blk = NA_Q_ROWS * GRID_W
    groups = t // blk
    steps = groups // ATTN_SUBSTEPS
    assert groups >= 3 and steps * ATTN_SUBSTEPS == groups and steps >= 2
    return pl.pallas_call(
        functools.partial(_natten_kernel, groups=groups),
        grid=(b, steps),
        in_specs=[pl.BlockSpec((None, w, ATTN_SUBSTEPS * blk), lambda bi, s: (bi, 0, s)),
                  pl.BlockSpec((None, t, w), lambda bi, s: (bi, 0, 0)),
                  pl.BlockSpec((None, w, t), lambda bi, s: (bi, 0, 0)),
                  _const_spec(bias_mid.shape),
                  pl.BlockSpec((None,) + bias_edge.shape[1:], lambda bi, s: (s // (steps - 1), 0, 0, 0))],
        out_specs=pl.BlockSpec((None, ATTN_SUBSTEPS * blk, w), lambda bi, s: (bi, s, 0)),
        out_shape=jax.ShapeDtypeStruct((b, t, w), _BF16),
        compiler_params=pltpu.CompilerParams(
            dimension_semantics=("parallel", "arbitrary"), vmem_limit_bytes=VMEM_LIMIT),
        name="natten",
    )(qt, k, vt, bias_mid, bias_edge)


def _swa_mask(key_pos0, query_pos0, xp=jnp):
    iota = lax.broadcasted_iota if xp is jnp else (lambda dt, shape, dim: np.indices(shape)[dim])
    kpos = key_pos0 + iota(jnp.int32, (SWA_K, SWA_Q), 0)
    qpos = query_pos0 + iota(jnp.int32, (SWA_K, SWA_Q), 1)
    return xp.where(xp.abs(qpos - kpos) <= WIN, 0.0, MASK_VALUE).astype(xp.float32)


def _swa_kernel(sink_ref, qt_ref, k_ref, vt_ref, mask_mid_ref, o_ref, *, seq):
    group = NB_HEADS // NB_KV_HEADS
    zeros = jnp.zeros((HEAD_DIM, SWA_Q), _BF16)

    def substep(qoff):
        q0 = pl.program_id(1) * (ATTN_SUBSTEPS * SWA_Q) + qoff
        start = pl.multiple_of(jnp.clip(q0 - WIN, 0, seq - SWA_K), WIN)

        def run(key_ranges, mask):
            k2 = k_ref[pl.ds(start, SWA_K), :]

            def scores(h):
                qh = qt_ref[h * HEAD_DIM:(h + 1) * HEAD_DIM, pl.ds(qoff, SWA_Q)]
                qh = jnp.concatenate([qh, zeros] if h // group == 0 else [zeros, qh], axis=0)
                return jnp.dot(k2, qh, preferred_element_type=_F32) + mask

            def values(h):
                j = h // group
                return vt_ref[j * HEAD_DIM:(j + 1) * HEAD_DIM, pl.ds(start, SWA_K)]

            def store_pair(h, o2):
                o_ref[pl.ds(qoff, SWA_Q), h * HEAD_DIM:(h + 2) * HEAD_DIM] = o2.T.astype(_BF16)

            _attend_heads(NB_HEADS, SWA_SCORES_AHEAD, scores, values, key_ranges, store_pair,
                          sink=lambda h: sink_ref[h] * LOG2E)

        n_tiles = SWA_Q // LANES
        mid_ranges = tuple((t * LANES, t * LANES + LANES + 2 * WIN) for t in range(n_tiles))
        mid = jnp.logical_and(q0 - WIN >= 0, q0 - WIN <= seq - SWA_K)
        pl.when(mid)(lambda: run(mid_ranges, mask_mid_ref[...]))
        pl.when(jnp.logical_not(mid))(lambda: run(((0, SWA_K),) * n_tiles, _swa_mask(start, q0)))

    _for_each_substep(ATTN_SUBSTEPS, SWA_Q, substep)


def _swa(qt, k, vt, sink):
    b, w, t = qt.shape
    assert k.shape[-1] == NB_KV_HEADS * HEAD_DIM == 2 * HEAD_DIM
    block = ATTN_SUBSTEPS * SWA_Q
    mask_mid = jnp.asarray(_swa_mask(0, WIN, xp=np))
    return pl.pallas_call(
        functools.partial(_swa_kernel, seq=t),
        grid=(b, t // block),
        in_specs=[pl.BlockSpec(memory_space=pltpu.SMEM),
                  pl.BlockSpec((None, w, block), lambda bi, n: (bi, 0, n)),
                  pl.BlockSpec((None, t, k.shape[-1]), lambda bi, n: (bi, 0, 0)),
                  pl.BlockSpec((None, vt.shape[1], t), lambda bi, n: (bi, 0, 0)),
                  _const_spec(mask_mid.shape)],
        out_specs=pl.BlockSpec((None, block, w), lambda bi, n: (bi, n, 0)),
        out_shape=jax.ShapeDtypeStruct((b, t, w), _BF16),
        compiler_params=pltpu.CompilerParams(
            dimension_semantics=("parallel", "arbitrary"), vmem_limit_bytes=VMEM_LIMIT),
        name="swa",
    )(sink, qt, k, vt, mask_mid)


def _merge_kernel(h_ref, ya_ref, yb_ref, ga_ref, gb_ref, wa_ref, wb_ref, wo_ref,
                  g2_ref, wg_ref, wu_ref, wd_ref, gf_ref, y_ref):
    rows = _sub_tiles(h_ref.shape[0])
    n_sub = len(rows)
    hs = []
    for r in rows:
        ma = jnp.dot(ya_ref[r, :], wa_ref[...], preferred_element_type=_F32)
        mb = jnp.dot(yb_ref[r, :], wb_ref[...], preferred_element_type=_F32)
        merged = ga_ref[r, :].astype(_F32) * ma + gb_ref[r, :].astype(_F32) * mb
        hs.append(h_ref[r, :] + jnp.dot(merged.astype(_BF16), wo_ref[...], preferred_element_type=_F32))

    normed = {0: _rms(hs[0], g2_ref[...]).astype(_BF16)}
    for i in range(n_sub):
        def norm_next(i=i):
            if i + 1 < n_sub:
                normed[i + 1] = _rms(hs[i + 1], g2_ref[...]).astype(_BF16)

        h = hs[i] + 0.5 * _swiglu(normed.pop(i), wg_ref, wu_ref, wd_ref, after_first_dots=norm_next)
        y_ref[rows[i], :] = _rms(h, gf_ref[...])


def _merge(h2d, ya, yb, ga, gb, wa, wb, wo, g2, wg, wu, wd, gf):
    n, d = h2d.shape

    def tile(width):
        return pl.BlockSpec((MERGE_BLOCK, width), lambda i: (i, 0))

    consts = (wa, wb, wo, g2, wg, wu, wd, gf)
    return pl.pallas_call(
        _merge_kernel,
        grid=(n // MERGE_BLOCK,),
        in_specs=[tile(d), tile(ya.shape[-1]), tile(yb.shape[-1]), tile(d), tile(d)]
                 + [_const_spec(c.shape) for c in consts],
        out_specs=tile(d),
        out_shape=jax.ShapeDtypeStruct((n, d), _F32),
        compiler_params=pltpu.CompilerParams(dimension_semantics=("parallel",), vmem_limit_bytes=VMEM_LIMIT),
        name="merge_ffn2",
    )(h2d, ya, yb, ga, gb, *consts)


def kernel(x, ffn1_norm, ffn1_w_gate, ffn1_w_up, ffn1_w_down, mix_norm, w_in, na_rpb, sink_logit, w_branch_a, w_branch_b, w_out, ffn2_norm, ffn2_w_gate, ffn2_w_up, ffn2_w_down, final_norm):
    b, t, d = x.shape
    assert ffn1_norm.shape[0] == 1, "single-layer trunk: the final norm is fused into the layer's last kernel"
    rows = t // GRID_W
    cos_t, sin_t = _rope_tables(t)
    bf = lambda w: w[0].astype(_BF16)
    row = lambda g: g.reshape(1, d).astype(_F32)
    seq3 = lambda a: a.reshape(b, t, a.shape[-1])

    h = _ffn1(x.reshape(b * t, d), row(ffn1_norm), bf(ffn1_w_gate), bf(ffn1_w_up), bf(ffn1_w_down))
    qat, ka, vat, qbt, kb, vbt, ga, gb = _inproj(h, row(mix_norm), bf(w_in), cos_t, sin_t, t)
    ya = _natten(qat, seq3(ka), vat, *_na_bias_tables(na_rpb[0].astype(_F32) * LOG2E, rows))
    yb = _swa(qbt, seq3(kb), vbt, sink_logit[0].astype(_F32))
    y = _merge(h, ya.reshape(b * t, -1), yb.reshape(b * t, -1), ga, gb,
               bf(w_branch_a), bf(w_branch_b), bf(w_out),
               row(ffn2_norm), bf(ffn2_w_gate), bf(ffn2_w_up), bf(ffn2_w_down), row(final_norm))
    return y.reshape(b, t, d)
```

```python
import functools

import jax
import jax.numpy as jnp
import numpy as np
from jax import lax
from jax.experimental import pallas as pl
from jax.experimental.pallas import tpu as pltpu

D_MODEL = 1024
HEAD_DIM = 64
NA_HEADS = 8
NB_HEADS = 8
NB_KV_HEADS = 2
GRID_W = 64
NA_KH = 8
NA_KW = 16
WIN = 128
ROPE_THETA = 10000.0
D_FF = 2816
EPS = 1e-6

WIDTH_A = NA_HEADS * HEAD_DIM
WIDTH_BQ = NB_HEADS * HEAD_DIM
WIDTH_BKV = NB_KV_HEADS * HEAD_DIM
LOG2E = 1.4426950408889634
Q_SCALE = HEAD_DIM ** -0.5 * LOG2E
MASK_VALUE = -1e30
LANES = 128
BF16_SUBLANES = 16

FFN1_BLOCK = 1024
MERGE_BLOCK = 1024
DENSE_SUB_TILE = 256
FF_CHUNKS = ((0, 1536), (1536, 2816))
NA_Q_ROWS = 4
NA_K_ROWS = NA_Q_ROWS + NA_KH - 1
SWA_Q = 256
SWA_K = SWA_Q + 2 * WIN
ATTN_SUBSTEPS = 4
NA_SCORES_AHEAD = 4
SWA_SCORES_AHEAD = 6
VMEM_LIMIT = 60 * 1024 * 1024

_BF16 = jnp.bfloat16
_F32 = jnp.float32


def _rms(x, g):
    return x * lax.rsqrt(jnp.mean(x * x, axis=-1, keepdims=True) + EPS) * g


def _sigmoid(x):
    return 1.0 / (1.0 + jnp.exp(-x))


def _swiglu(xb, wg_ref, wu_ref, wd_ref, after_first_dots=None):
    acc = None
    for c, (lo, hi) in enumerate(FF_CHUNKS):
        g = jnp.dot(xb, wg_ref[:, lo:hi], preferred_element_type=_F32)
        u = jnp.dot(xb, wu_ref[:, lo:hi], preferred_element_type=_F32)
        if c == 0 and after_first_dots is not None:
            after_first_dots()
        a = (g * _sigmoid(g) * u).astype(_BF16)
        part = jnp.dot(a, wd_ref[lo:hi, :], preferred_element_type=_F32)
        acc = part if acc is None else acc + part
    return acc


def _const_spec(shape):
    return pl.BlockSpec(shape, lambda *_: (0,) * len(shape), pipeline_mode=pl.Buffered(1))


def _sub_tiles(n_rows):
    return [slice(lo, lo + DENSE_SUB_TILE) for lo in range(0, n_rows, DENSE_SUB_TILE)]


def _for_each_substep(count, size, body):
    def step(i, carry):
        body(pl.multiple_of(i * size, size))
        return carry
    lax.fori_loop(0, count, step, 0)


def _ffn1_kernel(x_ref, g_ref, wg_ref, wu_ref, wd_ref, h_ref):
    rows = _sub_tiles(x_ref.shape[0])
    normed = {0: _rms(x_ref[rows[0], :], g_ref[...]).astype(_BF16)}

    for i in range(len(rows)):
        def norm_next(i=i):
            if i + 1 < len(rows):
                normed[i + 1] = _rms(x_ref[rows[i + 1], :], g_ref[...]).astype(_BF16)

        f = _swiglu(normed.pop(i), wg_ref, wu_ref, wd_ref, after_first_dots=norm_next)
        h_ref[rows[i], :] = x_ref[rows[i], :] + 0.5 * f


def _ffn1(x2d, gain, wg, wu, wd):
    n, d = x2d.shape
    block = FFN1_BLOCK
    tile = pl.BlockSpec((block, d), lambda i: (i, 0))
    return pl.pallas_call(
        _ffn1_kernel,
        grid=(n // block,),
        in_specs=[tile, _const_spec((1, d)), _const_spec(wg.shape), _const_spec(wu.shape), _const_spec(wd.shape)],
        out_specs=tile,
        out_shape=jax.ShapeDtypeStruct((n, d), _F32),
        compiler_params=pltpu.CompilerParams(dimension_semantics=("parallel",), vmem_limit_bytes=VMEM_LIMIT),
        name="ffn1",
    )(x2d, gain, wg, wu, wd)


_OFF_QA, _OFF_KA, _OFF_VA = 0, WIDTH_A, 2 * WIDTH_A
_OFF_QB = 3 * WIDTH_A
_OFF_KVB = _OFF_QB + WIDTH_BQ
_OFF_GA = _OFF_KVB + 2 * WIDTH_BKV
_OFF_GB = _OFF_GA + D_MODEL
D_IN = _OFF_GB + D_MODEL


def _rope(x, cos, sin_signed, first_half):
    w = x.shape[-1]
    half = HEAD_DIM // 2
    partner = jnp.where(first_half, pltpu.roll(x, w - half, 1), pltpu.roll(x, half, 1))
    return x * cos + partner * sin_signed


def _inproj_kernel(h_ref, g_ref, w_ref, cos_ref, sin_ref,
                   qat_ref, ka_ref, vat_ref, qbt_ref, kb_ref, vbt_ref, ga_ref, gb_ref):
    all_rows = _sub_tiles(h_ref.shape[0])
    n_sub = len(all_rows)
    normed = {0: _rms(h_ref[all_rows[0], :], g_ref[...]).astype(_BF16)}

    for i in range(n_sub):
        rows = all_rows[i]
        u = normed.pop(i)

        def proj(lo, hi):
            return jnp.dot(u, w_ref[:, lo:hi], preferred_element_type=_F32)

        qa = proj(_OFF_QA, _OFF_KA)
        if i + 1 < n_sub:
            normed[i + 1] = _rms(h_ref[all_rows[i + 1], :], g_ref[...]).astype(_BF16)
        qat_ref[:, rows] = (qa * Q_SCALE).T.astype(_BF16)
        ka_ref[rows, :] = proj(_OFF_KA, _OFF_VA).astype(_BF16)
        vat_ref[:, rows] = proj(_OFF_VA, _OFF_QB).T.astype(_BF16)

        cos = cos_ref[rows, :]
        sin = sin_ref[rows, :]
        lane = lax.broadcasted_iota(jnp.int32, cos.shape, 1)
        first = (lane % HEAD_DIM) < (HEAD_DIM // 2)
        reps = WIDTH_BQ // cos.shape[-1]
        cos_q = jnp.concatenate([cos] * reps, axis=-1)
        sin_q = jnp.concatenate([sin] * reps, axis=-1)
        first_q = jnp.concatenate([first] * reps, axis=-1)
        qb = _rope(proj(_OFF_QB, _OFF_KVB), cos_q, sin_q, first_q)
        qbt_ref[:, rows] = (qb * Q_SCALE).T.astype(_BF16)

        kv = proj(_OFF_KVB, _OFF_GA)
        kb_ref[rows, :] = _rope(kv[:, :WIDTH_BKV], cos, sin, first).astype(_BF16)
        vbt_ref[:, rows] = kv[:, WIDTH_BKV:].T.astype(_BF16)

        ga_ref[rows, :] = _sigmoid(proj(_OFF_GA, _OFF_GB)).astype(_BF16)
        gb_ref[rows, :] = _sigmoid(proj(_OFF_GB, D_IN)).astype(_BF16)


def _inproj(h2d, gain, w_in, cos_t, sin_t, seq):
    n, d = h2d.shape
    block = FFN1_BLOCK
    blocks_per_seq = seq // block

    def tile(width):
        return pl.BlockSpec((block, width), lambda i: (i, 0))

    def tposed(width):
        return pl.BlockSpec((None, width, block), lambda i: (i // blocks_per_seq, 0, i % blocks_per_seq))

    table = pl.BlockSpec((block, cos_t.shape[-1]), lambda i: (i % blocks_per_seq, 0))
    outs = [
        (True, WIDTH_A), (False, WIDTH_A), (True, WIDTH_A), (True, WIDTH_BQ), (False, WIDTH_BKV), (True, WIDTH_BKV),
        (False, D_MODEL), (False, D_MODEL)]
    return pl.pallas_call(
        _inproj_kernel,
        grid=(n // block,),
        in_specs=[tile(d), _const_spec((1, d)), _const_spec(w_in.shape), table, table],
        out_specs=[tposed(w) if t else tile(w) for t, w in outs],
        out_shape=[jax.ShapeDtypeStruct((n // seq, w, seq) if t else (n, w), _BF16) for t, w in outs],
        compiler_params=pltpu.CompilerParams(dimension_semantics=("parallel",), vmem_limit_bytes=VMEM_LIMIT),
        name="inproj",
    )(h2d, gain, w_in, cos_t, sin_t)


def _rope_tables(seq):
    half = HEAD_DIM // 2
    inv = ROPE_THETA ** (-jnp.arange(half, dtype=_F32) / half)
    ang = jnp.arange(seq).astype(_F32)[:, None] * inv[None, :]
    cos, sin = jnp.cos(ang), jnp.sin(ang)
    cos_h = jnp.concatenate([cos, cos], axis=-1)
    sin_h = jnp.concatenate([-sin, sin], axis=-1)
    return jnp.tile(cos_h, (1, 2)), jnp.tile(sin_h, (1, 2))


def _softmax_pv(s, vt, key_ranges, sink=None):
    nk, nq = s.shape
    probs, maxes = [], []
    for t, (lo, hi) in enumerate(key_ranges):
        band = s[lo:hi, t * LANES:(t + 1) * LANES]
        m = jnp.max(band, axis=0, keepdims=True)
        if sink is not None:
            m = jnp.maximum(m, sink[:, t * LANES:(t + 1) * LANES])
        e = jnp.exp2(band - m).astype(_BF16)
        pads = [jnp.zeros((n, LANES), _BF16) for n in (lo, nk - hi)]
        probs.append(jnp.concatenate([x for x in (pads[0], e, pads[1]) if x.shape[0]], axis=0))
        maxes.append(m)
    p = jnp.concatenate(probs, axis=1)
    vt_ones = jnp.concatenate([vt, jnp.ones((BF16_SUBLANES, nk), _BF16)], axis=0)
    ov = jnp.dot(vt_ones, p, preferred_element_type=_F32)
    l = ov[HEAD_DIM:HEAD_DIM + 1, :]
    if sink is not None:
        l = l + jnp.exp2(sink - jnp.concatenate(maxes, axis=1))
    return ov[:HEAD_DIM, :] * (1.0 / l)


def _attend_units(n_units, ahead, scores, values, key_ranges, emit, sink=None):
    pending = [scores(i) for i in range(ahead)]
    for i in range(n_units):
        if i + ahead < n_units:
            pending.append(scores(i + ahead))
        emit(i, _softmax_pv(pending.pop(0), values(i), key_ranges, None if sink is None else sink(i)))


def _na_bias_tables(rpb, rows):
    heads, n_dr, n_dc = rpb.shape
    span = 2 * GRID_W
    ring = jnp.concatenate(
        [rpb[..., NA_KW - 1:], jnp.zeros((heads, n_dr, span - n_dc), rpb.dtype), rpb[..., :NA_KW - 1]], axis=-1)
    skew = jnp.tile(ring, (1, 1, GRID_W))[..., :GRID_W * (span - 1)].reshape(heads, n_dr, GRID_W, span - 1)
    c = np.arange(GRID_W)
    cs = np.clip(c - NA_KW // 2, 0, GRID_W - NA_KW)
    col_ok = (c[None, :] >= cs[:, None]) & (c[None, :] < cs[:, None] + NA_KW)
    toe = jnp.where(col_ok, skew[..., :GRID_W], MASK_VALUE)
    toe = jnp.swapaxes(toe, -1, -2)
    masked = jnp.full((heads, GRID_W, GRID_W), MASK_VALUE, rpb.dtype)

    def tile(query_row0, key_row0, n_key_rows):
        key_blocks = []
        for kr in key_row0 + np.arange(n_key_rows):
            query_blocks = []
            for qr in query_row0 + np.arange(NA_Q_ROWS):
                rs = int(np.clip(qr - NA_KH // 2, 0, rows - NA_KH))
                inside = rs <= kr < rs + NA_KH
                query_blocks.append(toe[:, kr - qr + NA_KH - 1] if inside else masked)
            key_blocks.append(jnp.concatenate(query_blocks, axis=-1))
        return jnp.concatenate(key_blocks, axis=1)

    mid = tile(NA_Q_ROWS, NA_Q_ROWS - NA_KH // 2, NA_K_ROWS)
    edge = jnp.stack([tile(0, 0, NA_KH), tile(rows - NA_Q_ROWS, rows - NA_KH, NA_KH)])
    return mid, edge


def _natten_kernel(qt_ref, k_ref, vt_ref, bias_mid_ref, bias_edge_ref, o_ref, *, groups):
    blk = NA_Q_ROWS * GRID_W
    n_tiles = blk // LANES
    pair = 2 * HEAD_DIM
    upper = lax.broadcasted_iota(jnp.int32, (pair, blk), 0) >= HEAD_DIM

    def substep(qoff):
        a = pl.program_id(1) * ATTN_SUBSTEPS + qoff // blk

        def run(start, nk, key_ranges, bias_ref):
            def scores(h):
                rows = slice(h // 2 * pair, (h // 2 + 1) * pair)
                k2 = k_ref[pl.ds(start, nk), rows]
                q2 = qt_ref[rows, pl.ds(qoff, blk)]
                qh = jnp.where(upper if h % 2 else ~upper, q2, jnp.zeros_like(q2))
                return jnp.dot(k2, qh, preferred_element_type=_F32) + bias_ref[h]

            def values(h):
                return vt_ref[h * HEAD_DIM:(h + 1) * HEAD_DIM, pl.ds(start, nk)]

            outs = []

            def emit(h, o):
                outs.append(o)
                if h % 2:
                    o2 = jnp.concatenate(outs[-2:], axis=0)
                    o_ref[pl.ds(qoff, blk), (h - 1) * HEAD_DIM:(h + 1) * HEAD_DIM] = o2.T.astype(_BF16)

            _attend_units(NA_HEADS, NA_SCORES_AHEAD, scores, values, key_ranges, emit)

        rows_per_tile = LANES // GRID_W
        mid_ranges = tuple((t * rows_per_tile * GRID_W, (t * rows_per_tile + rows_per_tile + NA_KH - 1) * GRID_W)
                           for t in range(n_tiles))
        mid_start = pl.multiple_of((a * NA_Q_ROWS - NA_KH // 2) * GRID_W, blk)
        nk_edge = NA_KH * GRID_W
        edge_start = pl.multiple_of(jnp.where(a == 0, 0, groups * blk - nk_edge), blk)
        mid = jnp.logical_and(a >= 1, a <= groups - 2)
        pl.when(mid)(lambda: run(mid_start, NA_K_ROWS * GRID_W, mid_ranges, bias_mid_ref))
        pl.when(jnp.logical_not(mid))(lambda: run(edge_start, nk_edge, ((0, nk_edge),) * n_tiles, bias_edge_ref))

    _for_each_substep(ATTN_SUBSTEPS, blk, substep)


def _natten(qt, k, vt, bias_mid, bias_edge):
    b, t, w = k.shape
    blk = NA_Q_ROWS * GRID_W
    groups = t // blk
    steps = groups // ATTN_SUBSTEPS
    assert groups >= 3 and steps * ATTN_SUBSTEPS == groups and steps >= 2
    return pl.pallas_call(
        functools.partial(_natten_kernel, groups=groups),
        grid=(b, steps),
        in_specs=[pl.BlockSpec((None, w, ATTN_SUBSTEPS * blk), lambda bi, s: (bi, 0, s)),
                  pl.BlockSpec((None, t, w), lambda bi, s: (bi, 0, 0)),
                  pl.BlockSpec((None, w, t), lambda bi, s: (bi, 0, 0)),
                  _const_spec(bias_mid.shape),
                  pl.BlockSpec((None,) + bias_edge.shape[1:], lambda bi, s: (s // (steps - 1), 0, 0, 0))],
        out_specs=pl.BlockSpec((None, ATTN_SUBSTEPS * blk, w), lambda bi, s: (bi, s, 0)),
        out_shape=jax.ShapeDtypeStruct((b, t, w), _BF16),
        compiler_params=pltpu.CompilerParams(
            dimension_semantics=("parallel", "arbitrary"), vmem_limit_bytes=VMEM_LIMIT),
        name="natten",
    )(qt, k, vt, bias_mid, bias_edge)


def _swa_mask(key_pos0, query_pos0, xp=jnp):
    iota = lax.broadcasted_iota if xp is jnp else (lambda dt, shape, dim: np.indices(shape)[dim])
    kpos = key_pos0 + iota(jnp.int32, (SWA_K, SWA_Q), 0)
    qpos = query_pos0 + iota(jnp.int32, (SWA_K, SWA_Q), 1)
    return xp.where(xp.abs(qpos - kpos) <= WIN, 0.0, MASK_VALUE).astype(xp.float32)


def _swa_kernel(sink_ref, qt_ref, k_ref, vt_ref, mask_mid_ref, o_ref, *, seq):
    group = NB_HEADS // NB_KV_HEADS
    zeros = jnp.zeros((HEAD_DIM, SWA_Q), _BF16)

    def substep(qoff):
        q0 = pl.program_id(1) * (ATTN_SUBSTEPS * SWA_Q) + qoff
        start = pl.multiple_of(jnp.clip(q0 - WIN, 0, seq - SWA_K), WIN)

        def run(key_ranges, mask):
            k2 = k_ref[pl.ds(start, SWA_K), :]

            def scores(h):
                qh = qt_ref[h * HEAD_DIM:(h + 1) * HEAD_DIM, pl.ds(qoff, SWA_Q)]
                qh = jnp.concatenate([qh, zeros] if h // group == 0 else [zeros, qh], axis=0)
                return jnp.dot(k2, qh, preferred_element_type=_F32) + mask

            def values(h):
                j = h // group
                return vt_ref[j * HEAD_DIM:(j + 1) * HEAD_DIM, pl.ds(start, SWA_K)]

            def sink(h):
                return jnp.full((1, SWA_Q), sink_ref[h] * LOG2E, _F32)

            outs = []

            def emit(h, o):
                outs.append(o)
                if h % 2:
                    o2 = jnp.concatenate(outs[-2:], axis=0)
                    o_ref[pl.ds(qoff, SWA_Q), (h - 1) * HEAD_DIM:(h + 1) * HEAD_DIM] = o2.T.astype(_BF16)

            _attend_units(NB_HEADS, SWA_SCORES_AHEAD, scores, values, key_ranges, emit, sink=sink)

        n_tiles = SWA_Q // LANES
        mid_ranges = tuple((t * LANES, t * LANES + LANES + 2 * WIN) for t in range(n_tiles))
        mid = jnp.logical_and(q0 - WIN >= 0, q0 - WIN <= seq - SWA_K)
        pl.when(mid)(lambda: run(mid_ranges, mask_mid_ref[...]))
        pl.when(jnp.logical_not(mid))(lambda: run(((0, SWA_K),) * n_tiles, _swa_mask(start, q0)))

    _for_each_substep(ATTN_SUBSTEPS, SWA_Q, substep)


def _swa(qt, k, vt, sink):
    b, w, t = qt.shape
    assert k.shape[-1] == NB_KV_HEADS * HEAD_DIM == 2 * HEAD_DIM
    block = ATTN_SUBSTEPS * SWA_Q
    mask_mid = jnp.asarray(_swa_mask(0, WIN, xp=np))
    return pl.pallas_call(
        functools.partial(_swa_kernel, seq=t),
        grid=(b, t // block),
        in_specs=[pl.BlockSpec(memory_space=pltpu.SMEM),
                  pl.BlockSpec((None, w, block), lambda bi, n: (bi, 0, n)),
                  pl.BlockSpec((None, t, k.shape[-1]), lambda bi, n: (bi, 0, 0)),
                  pl.BlockSpec((None, vt.shape[1], t), lambda bi, n: (bi, 0, 0)),
                  _const_spec(mask_mid.shape)],
        out_specs=pl.BlockSpec((None, block, w), lambda bi, n: (bi, n, 0)),
        out_shape=jax.ShapeDtypeStruct((b, t, w), _BF16),
        compiler_params=pltpu.CompilerParams(
            dimension_semantics=("parallel", "arbitrary"), vmem_limit_bytes=VMEM_LIMIT),
        name="swa",
    )(sink, qt, k, vt, mask_mid)


def _merge_kernel(h_ref, ya_ref, yb_ref, ga_ref, gb_ref, wa_ref, wb_ref, wo_ref,
                  g2_ref, wg_ref, wu_ref, wd_ref, gf_ref, y_ref):
    rows = _sub_tiles(h_ref.shape[0])
    n_sub = len(rows)
    hs = []
    for r in rows:
        ma = jnp.dot(ya_ref[r, :], wa_ref[...], preferred_element_type=_F32)
        mb = jnp.dot(yb_ref[r, :], wb_ref[...], preferred_element_type=_F32)
        merged = ga_ref[r, :].astype(_F32) * ma + gb_ref[r, :].astype(_F32) * mb
        hs.append(h_ref[r, :] + jnp.dot(merged.astype(_BF16), wo_ref[...], preferred_element_type=_F32))

    normed = {0: _rms(hs[0], g2_ref[...]).astype(_BF16)}
    for i in range(n_sub):
        def norm_next(i=i):
            if i + 1 < n_sub:
                normed[i + 1] = _rms(hs[i + 1], g2_ref[...]).astype(_BF16)

        h = hs[i] + 0.5 * _swiglu(normed.pop(i), wg_ref, wu_ref, wd_ref, after_first_dots=norm_next)
        y_ref[rows[i], :] = _rms(h, gf_ref[...])


def _merge(h2d, ya, yb, ga, gb, wa, wb, wo, g2, wg, wu, wd, gf):
    n, d = h2d.shape

    def tile(width):
        return pl.BlockSpec((MERGE_BLOCK, width), lambda i: (i, 0))

    consts = (wa, wb, wo, g2, wg, wu, wd, gf)
    return pl.pallas_call(
        _merge_kernel,
        grid=(n // MERGE_BLOCK,),
        in_specs=[tile(d), tile(ya.shape[-1]), tile(yb.shape[-1]), tile(d), tile(d)]
                 + [_const_spec(c.shape) for c in consts],
        out_specs=tile(d),
        out_shape=jax.ShapeDtypeStruct((n, d), _F32),
        compiler_params=pltpu.CompilerParams(dimension_semantics=("parallel",), vmem_limit_bytes=VMEM_LIMIT),
        name="merge_ffn2",
    )(h2d, ya, yb, ga, gb, *consts)


def kernel(x, ffn1_norm, ffn1_w_gate, ffn1_w_up, ffn1_w_down, mix_norm, w_in, na_rpb, sink_logit, w_branch_a, w_branch_b, w_out, ffn2_norm, ffn2_w_gate, ffn2_w_up, ffn2_w_down, final_norm):
    b, t, d = x.shape
    assert ffn1_norm.shape[0] == 1, "single-layer trunk: the final norm is fused into the layer's last kernel"
    rows = t // GRID_W
    cos_t, sin_t = _rope_tables(t)
    bf = lambda w: w[0].astype(_BF16)
    row = lambda g: g.reshape(1, d).astype(_F32)
    seq3 = lambda a: a.reshape(b, t, a.shape[-1])

    h = _ffn1(x.reshape(b * t, d), row(ffn1_norm), bf(ffn1_w_gate), bf(ffn1_w_up), bf(ffn1_w_down))
    qat, ka, vat, qbt, kb, vbt, ga, gb = _inproj(h, row(mix_norm), bf(w_in), cos_t, sin_t, t)
    ya = _natten(qat, seq3(ka), vat, *_na_bias_tables(na_rpb[0].astype(_F32) * LOG2E, rows))
    yb = _swa(qbt, seq3(kb), vbt, sink_logit[0].astype(_F32))
    y = _merge(h, ya.reshape(b * t, -1), yb.reshape(b * t, -1), ga, gb,
               bf(w_branch_a), bf(w_branch_b), bf(w_out),
               row(ffn2_norm), bf(ffn2_w_gate), bf(ffn2_w_up), bf(ffn2_w_down), row(final_norm))
    return y.reshape(b, t, d)
```

```python
import functools

import jax
import jax.numpy as jnp
import numpy as np
from jax import lax
from jax.experimental import pallas as pl
from jax.experimental.pallas import tpu as pltpu

D_MODEL = 1024
HEAD_DIM = 64
NA_HEADS = 8
NB_HEADS = 8
NB_KV_HEADS = 2
GRID_W = 64
NA_KH = 8
NA_KW = 16
WIN = 128
ROPE_THETA = 10000.0
D_FF = 2816
EPS = 1e-6

WIDTH_A = NA_HEADS * HEAD_DIM
WIDTH_BQ = NB_HEADS * HEAD_DIM
WIDTH_BKV = NB_KV_HEADS * HEAD_DIM
LOG2E = 1.4426950408889634
Q_SCALE = HEAD_DIM ** -0.5 * LOG2E
MASK_VALUE = -1e30
LANES = 128
BF16_SUBLANES = 16
MXU_TILE = 256

DENSE_BLOCK = 1024
DENSE_SUB_TILE = 256
FF_SPLIT = 6 * MXU_TILE
FF_CHUNKS = ((0, FF_SPLIT), (FF_SPLIT, D_FF))
NA_Q_ROWS = 4
NA_K_ROWS = NA_Q_ROWS + NA_KH - 1
SWA_Q = 256
SWA_K = SWA_Q + 2 * WIN
ATTN_SUBSTEPS = 4
NA_SCORES_AHEAD = 4
SWA_SCORES_AHEAD = 6
VMEM_LIMIT = 60 * 1024 * 1024

_BF16 = jnp.bfloat16
_F32 = jnp.float32


def _rms(x, g):
    return x * lax.rsqrt(jnp.mean(x * x, axis=-1, keepdims=True) + EPS) * g


def _sigmoid(x):
    return 1.0 / (1.0 + jnp.exp(-x))


def _swiglu(xb, wg_ref, wu_ref, wd_ref, after_first_dots=None):
    assert wg_ref.shape[1] == wu_ref.shape[1] == wd_ref.shape[0] == FF_CHUNKS[-1][1]
    acc = None
    for c, (lo, hi) in enumerate(FF_CHUNKS):
        g = jnp.dot(xb, wg_ref[:, lo:hi], preferred_element_type=_F32)
        u = jnp.dot(xb, wu_ref[:, lo:hi], preferred_element_type=_F32)
        if c == 0 and after_first_dots is not None:
            after_first_dots()
        a = (g * _sigmoid(g) * u).astype(_BF16)
        part = jnp.dot(a, wd_ref[lo:hi, :], preferred_element_type=_F32)
        acc = part if acc is None else acc + part
    return acc


def _const_spec(shape):
    return pl.BlockSpec(shape, lambda *_: (0,) * len(shape), pipeline_mode=pl.Buffered(1))


def _sub_tiles(n_rows):
    return [slice(lo, lo + DENSE_SUB_TILE) for lo in range(0, n_rows, DENSE_SUB_TILE)]


def _for_each_substep(count, size, body):
    def step(i, carry):
        body(pl.multiple_of(i * size, size))
        return carry
    lax.fori_loop(0, count, step, 0)


def _ffn1_kernel(x_ref, g_ref, wg_ref, wu_ref, wd_ref, h_ref):
    rows = _sub_tiles(x_ref.shape[0])
    normed = {0: _rms(x_ref[rows[0], :], g_ref[...]).astype(_BF16)}

    for i in range(len(rows)):
        def norm_next(i=i):
            if i + 1 < len(rows):
                normed[i + 1] = _rms(x_ref[rows[i + 1], :], g_ref[...]).astype(_BF16)

        f = _swiglu(normed.pop(i), wg_ref, wu_ref, wd_ref, after_first_dots=norm_next)
        h_ref[rows[i], :] = x_ref[rows[i], :] + 0.5 * f


def _ffn1(x2d, gain, wg, wu, wd):
    n, d = x2d.shape
    block = DENSE_BLOCK
    tile = pl.BlockSpec((block, d), lambda i: (i, 0))
    return pl.pallas_call(
        _ffn1_kernel,
        grid=(n // block,),
        in_specs=[tile, _const_spec((1, d)), _const_spec(wg.shape), _const_spec(wu.shape), _const_spec(wd.shape)],
        out_specs=tile,
        out_shape=jax.ShapeDtypeStruct((n, d), _F32),
        compiler_params=pltpu.CompilerParams(dimension_semantics=("parallel",), vmem_limit_bytes=VMEM_LIMIT),
        name="ffn1",
    )(x2d, gain, wg, wu, wd)


_OFF_QA, _OFF_KA, _OFF_VA = 0, WIDTH_A, 2 * WIDTH_A
_OFF_QB = 3 * WIDTH_A
_OFF_KVB = _OFF_QB + WIDTH_BQ
_OFF_GA = _OFF_KVB + 2 * WIDTH_BKV
_OFF_GB = _OFF_GA + D_MODEL
D_IN = _OFF_GB + D_MODEL


def _rope(x, cos, sin_signed, first_half):
    w = x.shape[-1]
    half = HEAD_DIM // 2
    partner = jnp.where(first_half, pltpu.roll(x, w - half, 1), pltpu.roll(x, half, 1))
    return x * cos + partner * sin_signed


def _inproj_kernel(h_ref, g_ref, w_ref, cos_ref, sin_ref,
                   qat_ref, ka_ref, vat_ref, qbt_ref, kb_ref, vbt_ref, ga_ref, gb_ref):
    all_rows = _sub_tiles(h_ref.shape[0])
    n_sub = len(all_rows)
    normed = {0: _rms(h_ref[all_rows[0], :], g_ref[...]).astype(_BF16)}

    for i in range(n_sub):
        rows = all_rows[i]
        u = normed.pop(i)

        def proj(lo, hi):
            return jnp.dot(u, w_ref[:, lo:hi], preferred_element_type=_F32)

        qa = proj(_OFF_QA, _OFF_KA)
        if i + 1 < n_sub:
            normed[i + 1] = _rms(h_ref[all_rows[i + 1], :], g_ref[...]).astype(_BF16)
        qat_ref[:, rows] = (qa * Q_SCALE).T.astype(_BF16)
        ka_ref[rows, :] = proj(_OFF_KA, _OFF_VA).astype(_BF16)
        vat_ref[:, rows] = proj(_OFF_VA, _OFF_QB).T.astype(_BF16)

        cos = cos_ref[rows, :]
        sin = sin_ref[rows, :]
        lane = lax.broadcasted_iota(jnp.int32, cos.shape, 1)
        first = (lane % HEAD_DIM) < (HEAD_DIM // 2)
        reps = WIDTH_BQ // cos.shape[-1]
        cos_q = jnp.concatenate([cos] * reps, axis=-1)
        sin_q = jnp.concatenate([sin] * reps, axis=-1)
        first_q = jnp.concatenate([first] * reps, axis=-1)
        qb = _rope(proj(_OFF_QB, _OFF_KVB), cos_q, sin_q, first_q)
        qbt_ref[:, rows] = (qb * Q_SCALE).T.astype(_BF16)

        kv = proj(_OFF_KVB, _OFF_GA)
        kb_ref[rows, :] = _rope(kv[:, :WIDTH_BKV], cos, sin, first).astype(_BF16)
        vbt_ref[:, rows] = kv[:, WIDTH_BKV:].T.astype(_BF16)

        ga_ref[rows, :] = _sigmoid(proj(_OFF_GA, _OFF_GB)).astype(_BF16)
        gb_ref[rows, :] = _sigmoid(proj(_OFF_GB, D_IN)).astype(_BF16)


def _inproj(h2d, gain, w_in, cos_t, sin_t, seq):
    n, d = h2d.shape
    block = DENSE_BLOCK
    blocks_per_seq = seq // block

    def tile(width):
        return pl.BlockSpec((block, width), lambda i: (i, 0))

    def tposed(width):
        return pl.BlockSpec((None, width, block), lambda i: (i // blocks_per_seq, 0, i % blocks_per_seq))

    table = pl.BlockSpec((block, cos_t.shape[-1]), lambda i: (i % blocks_per_seq, 0))
    outs = [
        (True, WIDTH_A), (False, WIDTH_A), (True, WIDTH_A), (True, WIDTH_BQ), (False, WIDTH_BKV), (True, WIDTH_BKV),
        (False, D_MODEL), (False, D_MODEL)]
    return pl.pallas_call(
        _inproj_kernel,
        grid=(n // block,),
        in_specs=[tile(d), _const_spec((1, d)), _const_spec(w_in.shape), table, table],
        out_specs=[tposed(w) if t else tile(w) for t, w in outs],
        out_shape=[jax.ShapeDtypeStruct((n // seq, w, seq) if t else (n, w), _BF16) for t, w in outs],
        compiler_params=pltpu.CompilerParams(dimension_semantics=("parallel",), vmem_limit_bytes=VMEM_LIMIT),
        name="inproj",
    )(h2d, gain, w_in, cos_t, sin_t)


def _rope_tables(seq):
    half = HEAD_DIM // 2
    inv = ROPE_THETA ** (-jnp.arange(half, dtype=_F32) / half)
    ang = jnp.arange(seq).astype(_F32)[:, None] * inv[None, :]
    cos, sin = jnp.cos(ang), jnp.sin(ang)
    cos_h = jnp.concatenate([cos, cos], axis=-1)
    sin_h = jnp.concatenate([-sin, sin], axis=-1)
    return jnp.tile(cos_h, (1, 2)), jnp.tile(sin_h, (1, 2))


def _softmax_pv(s, vt, key_ranges, sink=None):
    nk, nq = s.shape
    probs, maxes = [], []
    for t, (lo, hi) in enumerate(key_ranges):
        band = s[lo:hi, t * LANES:(t + 1) * LANES]
        m = jnp.max(band, axis=0, keepdims=True)
        if sink is not None:
            m = jnp.maximum(m, sink[:, t * LANES:(t + 1) * LANES])
        e = jnp.exp2(band - m).astype(_BF16)
        pads = [jnp.zeros((n, LANES), _BF16) for n in (lo, nk - hi)]
        probs.append(jnp.concatenate([x for x in (pads[0], e, pads[1]) if x.shape[0]], axis=0))
        maxes.append(m)
    p = jnp.concatenate(probs, axis=1)
    vt_ones = jnp.concatenate([vt, jnp.ones((BF16_SUBLANES, nk), _BF16)], axis=0)
    ov = jnp.dot(vt_ones, p, preferred_element_type=_F32)
    l = ov[HEAD_DIM:HEAD_DIM + 1, :]
    if sink is not None:
        l = l + jnp.exp2(sink - jnp.concatenate(maxes, axis=1))
    return ov[:HEAD_DIM, :] * (1.0 / l)


def _attend_units(n_units, ahead, scores, values, key_ranges, emit, sink=None):
    pending = [scores(i) for i in range(ahead)]
    for i in range(n_units):
        if i + ahead < n_units:
            pending.append(scores(i + ahead))
        emit(i, _softmax_pv(pending.pop(0), values(i), key_ranges, None if sink is None else sink(i)))


def _na_bias_tables(rpb, rows):
    heads, n_dr, n_dc = rpb.shape
    span = 2 * GRID_W
    ring = jnp.concatenate(
        [rpb[..., NA_KW - 1:], jnp.zeros((heads, n_dr, span - n_dc), rpb.dtype), rpb[..., :NA_KW - 1]], axis=-1)
    skew = jnp.tile(ring, (1, 1, GRID_W))[..., :GRID_W * (span - 1)].reshape(heads, n_dr, GRID_W, span - 1)
    c = np.arange(GRID_W)
    cs = np.clip(c - NA_KW // 2, 0, GRID_W - NA_KW)
    col_ok = (c[None, :] >= cs[:, None]) & (c[None, :] < cs[:, None] + NA_KW)
    toe = jnp.where(col_ok, skew[..., :GRID_W], MASK_VALUE)
    toe = jnp.swapaxes(toe, -1, -2)
    masked = jnp.full((heads, GRID_W, GRID_W), MASK_VALUE, rpb.dtype)

    def tile(query_row0, key_row0, n_key_rows):
        key_blocks = []
        for kr in key_row0 + np.arange(n_key_rows):
            query_blocks = []
            for qr in query_row0 + np.arange(NA_Q_ROWS):
                rs = int(np.clip(qr - NA_KH // 2, 0, rows - NA_KH))
                inside = rs <= kr < rs + NA_KH
                query_blocks.append(toe[:, kr - qr + NA_KH - 1] if inside else masked)
            key_blocks.append(jnp.concatenate(query_blocks, axis=-1))
        return jnp.concatenate(key_blocks, axis=1)

    mid = tile(NA_Q_ROWS, NA_Q_ROWS - NA_KH // 2, NA_K_ROWS)
    edge = jnp.stack([tile(0, 0, NA_KH), tile(rows - NA_Q_ROWS, rows - NA_KH, NA_KH)])
    return mid, edge


def _natten_kernel(qt_ref, k_ref, vt_ref, bias_mid_ref, bias_edge_ref, o_ref, *, groups):
    blk = NA_Q_ROWS * GRID_W
    n_tiles = blk // LANES
    pair = 2 * HEAD_DIM
    upper = lax.broadcasted_iota(jnp.int32, (pair, blk), 0) >= HEAD_DIM

    def substep(qoff):
        a = pl.program_id(1) * ATTN_SUBSTEPS + qoff // blk

        def run(start, nk, key_ranges, bias_ref):
            def scores(h):
                rows = slice(h // 2 * pair, (h // 2 + 1) * pair)
                k2 = k_ref[pl.ds(start, nk), rows]
                q2 = qt_ref[rows, pl.ds(qoff, blk)]
                qh = jnp.where(upper if h % 2 else ~upper, q2, jnp.zeros_like(q2))
                return jnp.dot(k2, qh, preferred_element_type=_F32) + bias_ref[h]

            def values(h):
                return vt_ref[h * HEAD_DIM:(h + 1) * HEAD_DIM, pl.ds(start, nk)]

            outs = []

            def emit(h, o):
                outs.append(o)
                if h % 2:
                    o2 = jnp.concatenate(outs[-2:], axis=0)
                    o_ref[pl.ds(qoff, blk), (h - 1) * HEAD_DIM:(h + 1) * HEAD_DIM] = o2.T.astype(_BF16)

            _attend_units(NA_HEADS, NA_SCORES_AHEAD, scores, values, key_ranges, emit)

        rows_per_tile = LANES // GRID_W
        mid_ranges = tuple((t * rows_per_tile * GRID_W, (t * rows_per_tile + rows_per_tile + NA_KH - 1) * GRID_W)
                           for t in range(n_tiles))
        mid_start = pl.multiple_of((a * NA_Q_ROWS - NA_KH // 2) * GRID_W, blk)
        nk_edge = NA_KH * GRID_W
        edge_start = pl.multiple_of(jnp.where(a == 0, 0, groups * blk - nk_edge), blk)
        mid = jnp.logical_and(a >= 1, a <= groups - 2)
        pl.when(mid)(lambda: run(mid_start, NA_K_ROWS * GRID_W, mid_ranges, bias_mid_ref))
        pl.when(jnp.logical_not(mid))(lambda: run(edge_start, nk_edge, ((0, nk_edge),) * n_tiles, bias_edge_ref))

    _for_each_substep(ATTN_SUBSTEPS, blk, substep)


def _natten(qt, k, vt, bias_mid, bias_edge):
    b, t, w = k.shape
    blk = NA_Q_ROWS * GRID_W
    groups = t // blk
    steps = groups // ATTN_SUBSTEPS
    assert groups >= 3 and steps * ATTN_SUBSTEPS == groups and steps >= 2
    return pl.pallas_call(
        functools.partial(_natten_kernel, groups=groups),
        grid=(b, steps),
        in_specs=[pl.BlockSpec((None, w, ATTN_SUBSTEPS * blk), lambda bi, s: (bi, 0, s)),
                  pl.BlockSpec((None, t, w), lambda bi, s: (bi, 0, 0)),
                  pl.BlockSpec((None, w, t), lambda bi, s: (bi, 0, 0)),
                  _const_spec(bias_mid.shape),
                  pl.BlockSpec((None,) + bias_edge.shape[1:], lambda bi, s: (s // (steps - 1), 0, 0, 0))],
        out_specs=pl.BlockSpec((None, ATTN_SUBSTEPS * blk, w), lambda bi, s: (bi, s, 0)),
        out_shape=jax.ShapeDtypeStruct((b, t, w), _BF16),
        compiler_params=pltpu.CompilerParams(
            dimension_semantics=("parallel", "arbitrary"), vmem_limit_bytes=VMEM_LIMIT),
        name="natten",
    )(qt, k, vt, bias_mid, bias_edge)


def _swa_mask(key_pos0, query_pos0, xp=jnp):
    iota = lax.broadcasted_iota if xp is jnp else (lambda dt, shape, dim: np.indices(shape)[dim])
    kpos = key_pos0 + iota(jnp.int32, (SWA_K, SWA_Q), 0)
    qpos = query_pos0 + iota(jnp.int32, (SWA_K, SWA_Q), 1)
    return xp.where(xp.abs(qpos - kpos) <= WIN, 0.0, MASK_VALUE).astype(xp.float32)


def _swa_kernel(sink_ref, qt_ref, k_ref, vt_ref, mask_mid_ref, o_ref, *, seq):
    group = NB_HEADS // NB_KV_HEADS
    zeros = jnp.zeros((HEAD_DIM, SWA_Q), _BF16)

    def substep(qoff):
        q0 = pl.program_id(1) * (ATTN_SUBSTEPS * SWA_Q) + qoff
        start = pl.multiple_of(jnp.clip(q0 - WIN, 0, seq - SWA_K), WIN)

        def run(key_ranges, mask):
            k2 = k_ref[pl.ds(start, SWA_K), :]

            def scores(h):
                qh = qt_ref[h * HEAD_DIM:(h + 1) * HEAD_DIM, pl.ds(qoff, SWA_Q)]
                qh = jnp.concatenate([qh, zeros] if h // group == 0 else [zeros, qh], axis=0)
                return jnp.dot(k2, qh, preferred_element_type=_F32) + mask

            def values(h):
                j = h // group
                return vt_ref[j * HEAD_DIM:(j + 1) * HEAD_DIM, pl.ds(start, SWA_K)]

            def sink(h):
                return jnp.full((1, SWA_Q), sink_ref[h] * LOG2E, _F32)

            outs = []

            def emit(h, o):
                outs.append(o)
                if h % 2:
                    o2 = jnp.concatenate(outs[-2:], axis=0)
                    o_ref[pl.ds(qoff, SWA_Q), (h - 1) * HEAD_DIM:(h + 1) * HEAD_DIM] = o2.T.astype(_BF16)

            _attend_units(NB_HEADS, SWA_SCORES_AHEAD, scores, values, key_ranges, emit, sink=sink)

        n_tiles = SWA_Q // LANES
        mid_ranges = tuple((t * LANES, t * LANES + LANES + 2 * WIN) for t in range(n_tiles))
        mid = jnp.logical_and(q0 - WIN >= 0, q0 - WIN <= seq - SWA_K)
        pl.when(mid)(lambda: run(mid_ranges, mask_mid_ref[...]))
        pl.when(jnp.logical_not(mid))(lambda: run(((0, SWA_K),) * n_tiles, _swa_mask(start, q0)))

    _for_each_substep(ATTN_SUBSTEPS, SWA_Q, substep)


def _swa(qt, k, vt, sink):
    b, w, t = qt.shape
    assert k.shape[-1] == NB_KV_HEADS * HEAD_DIM == 2 * HEAD_DIM
    block = ATTN_SUBSTEPS * SWA_Q
    mask_mid = jnp.asarray(_swa_mask(0, WIN, xp=np))
    return pl.pallas_call(
        functools.partial(_swa_kernel, seq=t),
        grid=(b, t // block),
        in_specs=[pl.BlockSpec(memory_space=pltpu.SMEM),
                  pl.BlockSpec((None, w, block), lambda bi, n: (bi, 0, n)),
                  pl.BlockSpec((None, t, k.shape[-1]), lambda bi, n: (bi, 0, 0)),
                  pl.BlockSpec((None, vt.shape[1], t), lambda bi, n: (bi, 0, 0)),
                  _const_spec(mask_mid.shape)],
        out_specs=pl.BlockSpec((None, block, w), lambda bi, n: (bi, n, 0)),
        out_shape=jax.ShapeDtypeStruct((b, t, w), _BF16),
        compiler_params=pltpu.CompilerParams(
            dimension_semantics=("parallel", "arbitrary"), vmem_limit_bytes=VMEM_LIMIT),
        name="swa",
    )(sink, qt, k, vt, mask_mid)


def _merge_kernel(h_ref, ya_ref, yb_ref, ga_ref, gb_ref, wa_ref, wb_ref, wo_ref,
                  g2_ref, wg_ref, wu_ref, wd_ref, gf_ref, y_ref):
    rows = _sub_tiles(h_ref.shape[0])
    n_sub = len(rows)
    hs = []
    for r in rows:
        ma = jnp.dot(ya_ref[r, :], wa_ref[...], preferred_element_type=_F32)
        mb = jnp.dot(yb_ref[r, :], wb_ref[...], preferred_element_type=_F32)
        merged = ga_ref[r, :].astype(_F32) * ma + gb_ref[r, :].astype(_F32) * mb
        hs.append(h_ref[r, :] + jnp.dot(merged.astype(_BF16), wo_ref[...], preferred_element_type=_F32))

    normed = {0: _rms(hs[0], g2_ref[...]).astype(_BF16)}
    for i in range(n_sub):
        def norm_next(i=i):
            if i + 1 < n_sub:
                normed[i + 1] = _rms(hs[i + 1], g2_ref[...]).astype(_BF16)

        h = hs[i] + 0.5 * _swiglu(normed.pop(i), wg_ref, wu_ref, wd_ref, after_first_dots=norm_next)
        y_ref[rows[i], :] = _rms(h, gf_ref[...])


def _merge(h2d, ya, yb, ga, gb, wa, wb, wo, g2, wg, wu, wd, gf):
    n, d = h2d.shape

    def tile(width):
        return pl.BlockSpec((DENSE_BLOCK, width), lambda i: (i, 0))

    consts = (wa, wb, wo, g2, wg, wu, wd, gf)
    return pl.pallas_call(
        _merge_kernel,
        grid=(n // DENSE_BLOCK,),
        in_specs=[tile(d), tile(ya.shape[-1]), tile(yb.shape[-1]), tile(d), tile(d)]
                 + [_const_spec(c.shape) for c in consts],
        out_specs=tile(d),
        out_shape=jax.ShapeDtypeStruct((n, d), _F32),
        compiler_params=pltpu.CompilerParams(dimension_semantics=("parallel",), vmem_limit_bytes=VMEM_LIMIT),
        name="merge_ffn2",
    )(h2d, ya, yb, ga, gb, *consts)


def kernel(x, ffn1_norm, ffn1_w_gate, ffn1_w_up, ffn1_w_down, mix_norm, w_in, na_rpb, sink_logit, w_branch_a, w_branch_b, w_out, ffn2_norm, ffn2_w_gate, ffn2_w_up, ffn2_w_down, final_norm):
    b, t, d = x.shape
    assert ffn1_norm.shape[0] == 1, "single-layer trunk: the final norm is fused into the layer's last kernel"
    rows = t // GRID_W
    cos_t, sin_t = _rope_tables(t)
    bf = lambda w: w[0].astype(_BF16)
    row = lambda g: g.reshape(1, d).astype(_F32)
    seq3 = lambda a: a.reshape(b, t, a.shape[-1])

    h = _ffn1(x.reshape(b * t, d), row(ffn1_norm), bf(ffn1_w_gate), bf(ffn1_w_up), bf(ffn1_w_down))
    qat, ka, vat, qbt, kb, vbt, ga, gb = _inproj(h, row(mix_norm), bf(w_in), cos_t, sin_t, t)
    ya = _natten(qat, seq3(ka), vat, *_na_bias_tables(na_rpb[0].astype(_F32) * LOG2E, rows))
    yb = _swa(qbt, seq3(kb), vbt, sink_logit[0].astype(_F32))
    y = _merge(h, ya.reshape(b * t, -1), yb.reshape(b * t, -1), ga, gb,
               bf(w_branch_a), bf(w_branch_b), bf(w_out),
               row(ffn2_norm), bf(ffn2_w_gate), bf(ffn2_w_up), bf(ffn2_w_down), row(final_norm))
    return y.reshape(b, t, d)
```

```python
import functools

import jax
import jax.numpy as jnp
import numpy as np
from jax import lax
from jax.experimental import pallas as pl
from jax.experimental.pallas import tpu as pltpu

D_MODEL = 1024
HEAD_DIM = 64
NA_HEADS = 8
NB_HEADS = 8
NB_KV_HEADS = 2
GRID_W = 64
NA_KH = 8
NA_KW = 16
WIN = 128
ROPE_THETA = 10000.0
D_FF = 2816
EPS = 1e-6

WIDTH_A = NA_HEADS * HEAD_DIM
WIDTH_BQ = NB_HEADS * HEAD_DIM
WIDTH_BKV = NB_KV_HEADS * HEAD_DIM
LOG2E = 1.4426950408889634
Q_SCALE = HEAD_DIM ** -0.5 * LOG2E
MASK_VALUE = -1e30
LANES = 128
BF16_SUBLANES = 16
MXU_TILE = 256

DENSE_BLOCK = 1024
DENSE_SUB_TILE = 256
FF_SPLIT = 6 * MXU_TILE
FF_CHUNKS = ((0, FF_SPLIT), (FF_SPLIT, D_FF))
NA_Q_ROWS = 4
NA_K_ROWS = NA_Q_ROWS + NA_KH - 1
SWA_Q = 256
SWA_K = SWA_Q + 2 * WIN
ATTN_SUBSTEPS = 4
MIX_SCORES_AHEAD = 8
VMEM_LIMIT = 60 * 1024 * 1024

_BF16 = jnp.bfloat16
_F32 = jnp.float32


def _rms(x, g):
    return x * lax.rsqrt(jnp.mean(x * x, axis=-1, keepdims=True) + EPS) * g


def _sigmoid(x):
    return 1.0 / (1.0 + jnp.exp(-x))


def _swiglu(xb, wg_ref, wu_ref, wd_ref, after_first_dots=None):
    assert wg_ref.shape[1] == wu_ref.shape[1] == wd_ref.shape[0] == FF_CHUNKS[-1][1]
    acc = None
    for c, (lo, hi) in enumerate(FF_CHUNKS):
        g = jnp.dot(xb, wg_ref[:, lo:hi], preferred_element_type=_F32)
        u = jnp.dot(xb, wu_ref[:, lo:hi], preferred_element_type=_F32)
        if c == 0 and after_first_dots is not None:
            after_first_dots()
        a = (g * _sigmoid(g) * u).astype(_BF16)
        part = jnp.dot(a, wd_ref[lo:hi, :], preferred_element_type=_F32)
        acc = part if acc is None else acc + part
    return acc


def _const_spec(shape):
    return pl.BlockSpec(shape, lambda *_: (0,) * len(shape), pipeline_mode=pl.Buffered(1))


def _sub_tiles(n_rows):
    return [slice(lo, lo + DENSE_SUB_TILE) for lo in range(0, n_rows, DENSE_SUB_TILE)]


def _for_each_substep(count, size, body):
    def step(i, carry):
        body(pl.multiple_of(i * size, size))
        return carry
    lax.fori_loop(0, count, step, 0)


def _ffn1_kernel(x_ref, g_ref, wg_ref, wu_ref, wd_ref, h_ref):
    rows = _sub_tiles(x_ref.shape[0])
    normed = {0: _rms(x_ref[rows[0], :], g_ref[...]).astype(_BF16)}

    for i in range(len(rows)):
        def norm_next(i=i):
            if i + 1 < len(rows):
                normed[i + 1] = _rms(x_ref[rows[i + 1], :], g_ref[...]).astype(_BF16)

        f = _swiglu(normed.pop(i), wg_ref, wu_ref, wd_ref, after_first_dots=norm_next)
        h_ref[rows[i], :] = x_ref[rows[i], :] + 0.5 * f


def _ffn1(x2d, gain, wg, wu, wd):
    n, d = x2d.shape
    block = DENSE_BLOCK
    tile = pl.BlockSpec((block, d), lambda i: (i, 0))
    return pl.pallas_call(
        _ffn1_kernel,
        grid=(n // block,),
        in_specs=[tile, _const_spec((1, d)), _const_spec(wg.shape), _const_spec(wu.shape), _const_spec(wd.shape)],
        out_specs=tile,
        out_shape=jax.ShapeDtypeStruct((n, d), _F32),
        compiler_params=pltpu.CompilerParams(dimension_semantics=("parallel",), vmem_limit_bytes=VMEM_LIMIT),
        name="ffn1",
    )(x2d, gain, wg, wu, wd)


_OFF_QA, _OFF_KA, _OFF_VA = 0, WIDTH_A, 2 * WIDTH_A
_OFF_QB = 3 * WIDTH_A
_OFF_KVB = _OFF_QB + WIDTH_BQ
_OFF_GA = _OFF_KVB + 2 * WIDTH_BKV
_OFF_GB = _OFF_GA + D_MODEL
D_IN = _OFF_GB + D_MODEL


def _rope(x, cos, sin_signed, first_half):
    w = x.shape[-1]
    half = HEAD_DIM // 2
    partner = jnp.where(first_half, pltpu.roll(x, w - half, 1), pltpu.roll(x, half, 1))
    return x * cos + partner * sin_signed


def _inproj_kernel(h_ref, g_ref, w_ref, cos_ref, sin_ref,
                   qat_ref, ka_ref, vat_ref, qbt_ref, kb_ref, vbt_ref, ga_ref, gb_ref):
    all_rows = _sub_tiles(h_ref.shape[0])
    n_sub = len(all_rows)
    normed = {0: _rms(h_ref[all_rows[0], :], g_ref[...]).astype(_BF16)}

    for i in range(n_sub):
        rows = all_rows[i]
        u = normed.pop(i)

        def proj(lo, hi):
            return jnp.dot(u, w_ref[:, lo:hi], preferred_element_type=_F32)

        qa = proj(_OFF_QA, _OFF_KA)
        if i + 1 < n_sub:
            normed[i + 1] = _rms(h_ref[all_rows[i + 1], :], g_ref[...]).astype(_BF16)
        qat_ref[:, rows] = (qa * Q_SCALE).T.astype(_BF16)
        ka_ref[rows, :] = proj(_OFF_KA, _OFF_VA).astype(_BF16)
        vat_ref[:, rows] = proj(_OFF_VA, _OFF_QB).T.astype(_BF16)

        cos = cos_ref[rows, :]
        sin = sin_ref[rows, :]
        lane = lax.broadcasted_iota(jnp.int32, cos.shape, 1)
        first = (lane % HEAD_DIM) < (HEAD_DIM // 2)
        reps = WIDTH_BQ // cos.shape[-1]
        cos_q = jnp.concatenate([cos] * reps, axis=-1)
        sin_q = jnp.concatenate([sin] * reps, axis=-1)
        first_q = jnp.concatenate([first] * reps, axis=-1)
        qb = _rope(proj(_OFF_QB, _OFF_KVB), cos_q, sin_q, first_q)
        qbt_ref[:, rows] = (qb * Q_SCALE).T.astype(_BF16)

        kv = proj(_OFF_KVB, _OFF_GA)
        kb_ref[rows, :] = _rope(kv[:, :WIDTH_BKV], cos, sin, first).astype(_BF16)
        vbt_ref[:, rows] = kv[:, WIDTH_BKV:].T.astype(_BF16)

        ga_ref[rows, :] = _sigmoid(proj(_OFF_GA, _OFF_GB)).astype(_BF16)
        gb_ref[rows, :] = _sigmoid(proj(_OFF_GB, D_IN)).astype(_BF16)


def _inproj(h2d, gain, w_in, cos_t, sin_t, seq):
    n, d = h2d.shape
    block = DENSE_BLOCK
    blocks_per_seq = seq // block

    def tile(width):
        return pl.BlockSpec((block, width), lambda i: (i, 0))

    def tposed(width):
        return pl.BlockSpec((None, width, block), lambda i: (i // blocks_per_seq, 0, i % blocks_per_seq))

    table = pl.BlockSpec((block, cos_t.shape[-1]), lambda i: (i % blocks_per_seq, 0))
    outs = [
        (True, WIDTH_A), (False, WIDTH_A), (True, WIDTH_A), (True, WIDTH_BQ), (False, WIDTH_BKV), (True, WIDTH_BKV),
        (False, D_MODEL), (False, D_MODEL)]
    return pl.pallas_call(
        _inproj_kernel,
        grid=(n // block,),
        in_specs=[tile(d), _const_spec((1, d)), _const_spec(w_in.shape), table, table],
        out_specs=[tposed(w) if t else tile(w) for t, w in outs],
        out_shape=[jax.ShapeDtypeStruct((n // seq, w, seq) if t else (n, w), _BF16) for t, w in outs],
        compiler_params=pltpu.CompilerParams(dimension_semantics=("parallel",), vmem_limit_bytes=VMEM_LIMIT),
        name="inproj",
    )(h2d, gain, w_in, cos_t, sin_t)


def _rope_tables(seq):
    half = HEAD_DIM // 2
    inv = ROPE_THETA ** (-jnp.arange(half, dtype=_F32) / half)
    ang = jnp.arange(seq).astype(_F32)[:, None] * inv[None, :]
    cos, sin = jnp.cos(ang), jnp.sin(ang)
    cos_h = jnp.concatenate([cos, cos], axis=-1)
    sin_h = jnp.concatenate([-sin, sin], axis=-1)
    return jnp.tile(cos_h, (1, 2)), jnp.tile(sin_h, (1, 2))


def _softmax_pv(s, vt, key_ranges, sink=None):
    nk, nq = s.shape
    probs, maxes = [], []
    for t, (lo, hi) in enumerate(key_ranges):
        band = s[lo:hi, t * LANES:(t + 1) * LANES]
        m = jnp.max(band, axis=0, keepdims=True)
        if sink is not None:
            m = jnp.maximum(m, sink[:, t * LANES:(t + 1) * LANES])
        e = jnp.exp2(band - m).astype(_BF16)
        pads = [jnp.zeros((n, LANES), _BF16) for n in (lo, nk - hi)]
        probs.append(jnp.concatenate([x for x in (pads[0], e, pads[1]) if x.shape[0]], axis=0))
        maxes.append(m)
    p = jnp.concatenate(probs, axis=1)
    vt_ones = jnp.concatenate([vt, jnp.ones((BF16_SUBLANES, nk), _BF16)], axis=0)
    ov = jnp.dot(vt_ones, p, preferred_element_type=_F32)
    l = ov[HEAD_DIM:HEAD_DIM + 1, :]
    if sink is not None:
        l = l + jnp.exp2(sink - jnp.concatenate(maxes, axis=1))
    return ov[:HEAD_DIM, :] * (1.0 / l)


def _attend_units(units, ahead):
    pending = [u[0]() for u in units[:ahead]]
    for i, (_, values, key_ranges, emit, sink) in enumerate(units):
        if i + ahead < len(units):
            pending.append(units[i + ahead][0]())
        emit(_softmax_pv(pending.pop(0), values(), key_ranges, None if sink is None else sink()))


def _na_bias_tables(rpb, rows):
    heads, n_dr, n_dc = rpb.shape
    span = 2 * GRID_W
    ring = jnp.concatenate(
        [rpb[..., NA_KW - 1:], jnp.zeros((heads, n_dr, span - n_dc), rpb.dtype), rpb[..., :NA_KW - 1]], axis=-1)
    skew = jnp.tile(ring, (1, 1, GRID_W))[..., :GRID_W * (span - 1)].reshape(heads, n_dr, GRID_W, span - 1)
    c = np.arange(GRID_W)
    cs = np.clip(c - NA_KW // 2, 0, GRID_W - NA_KW)
    col_ok = (c[None, :] >= cs[:, None]) & (c[None, :] < cs[:, None] + NA_KW)
    toe = jnp.where(col_ok, skew[..., :GRID_W], MASK_VALUE)
    toe = jnp.swapaxes(toe, -1, -2)
    masked = jnp.full((heads, GRID_W, GRID_W), MASK_VALUE, rpb.dtype)

    def tile(query_row0, key_row0, n_key_rows):
        key_blocks = []
        for kr in key_row0 + np.arange(n_key_rows):
            query_blocks = []
            for qr in query_row0 + np.arange(NA_Q_ROWS):
                rs = int(np.clip(qr - NA_KH // 2, 0, rows - NA_KH))
                inside = rs <= kr < rs + NA_KH
                query_blocks.append(toe[:, kr - qr + NA_KH - 1] if inside else masked)
            key_blocks.append(jnp.concatenate(query_blocks, axis=-1))
        return jnp.concatenate(key_blocks, axis=1)

    mid = tile(NA_Q_ROWS, NA_Q_ROWS - NA_KH // 2, NA_K_ROWS)
    edge = jnp.stack([tile(0, 0, NA_KH), tile(rows - NA_Q_ROWS, rows - NA_KH, NA_KH)])
    return mid, edge


def _swa_mask(key_pos0, query_pos0, xp=jnp):
    iota = lax.broadcasted_iota if xp is jnp else (lambda dt, shape, dim: np.indices(shape)[dim])
    kpos = key_pos0 + iota(jnp.int32, (SWA_K, SWA_Q), 0)
    qpos = query_pos0 + iota(jnp.int32, (SWA_K, SWA_Q), 1)
    return xp.where(xp.abs(qpos - kpos) <= WIN, 0.0, MASK_VALUE).astype(xp.float32)


def _mixers_kernel(sink_ref, qat_ref, ka_ref, vat_ref, bias_mid_ref, bias_edge_ref,
                   qbt_ref, kb_ref, vbt_ref, mask_mid_ref, oa_ref, ob_ref, *, groups):
    blk = NA_Q_ROWS * GRID_W
    seq = groups * blk
    n_tiles = blk // LANES
    pair = 2 * HEAD_DIM
    upper = lax.broadcasted_iota(jnp.int32, (pair, blk), 0) >= HEAD_DIM
    group = NB_HEADS // NB_KV_HEADS
    zeros = jnp.zeros((HEAD_DIM, blk), _BF16)

    def pair_store(o_ref, qoff):
        outs = []

        def emit(h, o):
            outs.append(o)
            if h % 2:
                o2 = jnp.concatenate(outs[-2:], axis=0)
                o_ref[pl.ds(qoff, blk), (h - 1) * HEAD_DIM:(h + 1) * HEAD_DIM] = o2.T.astype(_BF16)
        return emit

    def substep(qoff):
        a = pl.program_id(1) * ATTN_SUBSTEPS + qoff // blk
        q0 = a * blk

        def natten_units(start, nk, key_ranges, bias_ref):
            def scores(h):
                rows = slice(h // 2 * pair, (h // 2 + 1) * pair)
                k2 = ka_ref[pl.ds(start, nk), rows]
                q2 = qat_ref[rows, pl.ds(qoff, blk)]
                qh = jnp.where(upper if h % 2 else ~upper, q2, jnp.zeros_like(q2))
                return jnp.dot(k2, qh, preferred_element_type=_F32) + bias_ref[h]

            def values(h):
                return vat_ref[h * HEAD_DIM:(h + 1) * HEAD_DIM, pl.ds(start, nk)]

            emit = pair_store(oa_ref, qoff)
            return [(functools.partial(scores, h), functools.partial(values, h), key_ranges,
                     functools.partial(emit, h), None) for h in range(NA_HEADS)]

        def swa_units(key_ranges, mask_fn):
            start = pl.multiple_of(jnp.clip(q0 - WIN, 0, seq - SWA_K), WIN)

            def scores(h):
                k2 = kb_ref[pl.ds(start, SWA_K), :]
                qh = qbt_ref[h * HEAD_DIM:(h + 1) * HEAD_DIM, pl.ds(qoff, blk)]
                qh = jnp.concatenate([qh, zeros] if h // group == 0 else [zeros, qh], axis=0)
                return jnp.dot(k2, qh, preferred_element_type=_F32) + mask_fn(start)

            def values(h):
                j = h // group
                return vbt_ref[j * HEAD_DIM:(j + 1) * HEAD_DIM, pl.ds(start, SWA_K)]

            def sink(h):
                return jnp.full((1, blk), sink_ref[h] * LOG2E, _F32)

            emit = pair_store(ob_ref, qoff)
            return [(functools.partial(scores, h), functools.partial(values, h), key_ranges,
                     functools.partial(emit, h), functools.partial(sink, h)) for h in range(NB_HEADS)]

        def run(na_units, sw_units):
            _attend_units([u for both in zip(na_units, sw_units) for u in both], MIX_SCORES_AHEAD)

        rows_per_tile = LANES // GRID_W
        na_mid_ranges = tuple((t * rows_per_tile * GRID_W, (t * rows_per_tile + rows_per_tile + NA_KH - 1) * GRID_W)
                              for t in range(n_tiles))
        na_mid_start = pl.multiple_of((a * NA_Q_ROWS - NA_KH // 2) * GRID_W, blk)
        nk_edge = NA_KH * GRID_W
        na_edge_start = pl.multiple_of(jnp.where(a == 0, 0, seq - nk_edge), blk)
        sw_mid_ranges = tuple((t * LANES, t * LANES + LANES + 2 * WIN) for t in range(n_tiles))
        mid = jnp.logical_and(a >= 1, a <= groups - 2)
        pl.when(mid)(lambda: run(
            natten_units(na_mid_start, NA_K_ROWS * GRID_W, na_mid_ranges, bias_mid_ref),
            swa_units(sw_mid_ranges, lambda start: mask_mid_ref[...])))
        pl.when(jnp.logical_not(mid))(lambda: run(
            natten_units(na_edge_start, nk_edge, ((0, nk_edge),) * n_tiles, bias_edge_ref),
            swa_units(((0, SWA_K),) * n_tiles, lambda start: _swa_mask(start, q0))))

    _for_each_substep(ATTN_SUBSTEPS, blk, substep)


def _mixers(qat, ka, vat, bias_mid, bias_edge, qbt, kb, vbt, sink):
    b, t, w = ka.shape
    blk = NA_Q_ROWS * GRID_W
    groups = t // blk
    steps = groups // ATTN_SUBSTEPS
    assert blk == SWA_Q and NB_HEADS == NA_HEADS and kb.shape[-1] == 2 * HEAD_DIM
    assert groups >= 3 and steps * ATTN_SUBSTEPS == groups and steps >= 2
    mask_mid = jnp.asarray(_swa_mask(0, WIN, xp=np))
    block = ATTN_SUBSTEPS * blk
    q_spec = pl.BlockSpec((None, w, block), lambda bi, s: (bi, 0, s))
    o_spec = pl.BlockSpec((None, block, w), lambda bi, s: (bi, s, 0))

    def whole_seq(shape, **kw):
        return pl.BlockSpec((None,) + shape, lambda bi, s: (bi, 0, 0), **kw)

    return pl.pallas_call(
        functools.partial(_mixers_kernel, groups=groups),
        grid=(b, steps),
        in_specs=[pl.BlockSpec(memory_space=pltpu.SMEM),
                  q_spec,
                  whole_seq((t, w), pipeline_mode=pl.Buffered(1)),
                  whole_seq((w, t), pipeline_mode=pl.Buffered(1)),
                  _const_spec(bias_mid.shape),
                  pl.BlockSpec((None,) + bias_edge.shape[1:], lambda bi, s: (s // (steps - 1), 0, 0, 0)),
                  q_spec,
                  whole_seq((t, kb.shape[-1])),
                  whole_seq((vbt.shape[1], t)),
                  _const_spec(mask_mid.shape)],
        out_specs=[o_spec, o_spec],
        out_shape=[jax.ShapeDtypeStruct((b, t, w), _BF16)] * 2,
        compiler_params=pltpu.CompilerParams(
            dimension_semantics=("parallel", "arbitrary"), vmem_limit_bytes=VMEM_LIMIT),
        name="mixers",
    )(sink, qat, ka, vat, bias_mid, bias_edge, qbt, kb, vbt, mask_mid)


def _merge_kernel(h_ref, ya_ref, yb_ref, ga_ref, gb_ref, wa_ref, wb_ref, wo_ref,
                  g2_ref, wg_ref, wu_ref, wd_ref, gf_ref, y_ref):
    rows = _sub_tiles(h_ref.shape[0])
    n_sub = len(rows)
    hs = []
    for r in rows:
        ma = jnp.dot(ya_ref[r, :], wa_ref[...], preferred_element_type=_F32)
        mb = jnp.dot(yb_ref[r, :], wb_ref[...], preferred_element_type=_F32)
        merged = ga_ref[r, :].astype(_F32) * ma + gb_ref[r, :].astype(_F32) * mb
        hs.append(h_ref[r, :] + jnp.dot(merged.astype(_BF16), wo_ref[...], preferred_element_type=_F32))

    normed = {0: _rms(hs[0], g2_ref[...]).astype(_BF16)}
    for i in range(n_sub):
        def norm_next(i=i):
            if i + 1 < n_sub:
                normed[i + 1] = _rms(hs[i + 1], g2_ref[...]).astype(_BF16)

        h = hs[i] + 0.5 * _swiglu(normed.pop(i), wg_ref, wu_ref, wd_ref, after_first_dots=norm_next)
        y_ref[rows[i], :] = _rms(h, gf_ref[...])


def _merge(h2d, ya, yb, ga, gb, wa, wb, wo, g2, wg, wu, wd, gf):
    n, d = h2d.shape

    def tile(width):
        return pl.BlockSpec((DENSE_BLOCK, width), lambda i: (i, 0))

    consts = (wa, wb, wo, g2, wg, wu, wd, gf)
    return pl.pallas_call(
        _merge_kernel,
        grid=(n // DENSE_BLOCK,),
        in_specs=[tile(d), tile(ya.shape[-1]), tile(yb.shape[-1]), tile(d), tile(d)]
                 + [_const_spec(c.shape) for c in consts],
        out_specs=tile(d),
        out_shape=jax.ShapeDtypeStruct((n, d), _F32),
        compiler_params=pltpu.CompilerParams(dimension_semantics=("parallel",), vmem_limit_bytes=VMEM_LIMIT),
        name="merge_ffn2",
    )(h2d, ya, yb, ga, gb, *consts)


def kernel(x, ffn1_norm, ffn1_w_gate, ffn1_w_up, ffn1_w_down, mix_norm, w_in, na_rpb, sink_logit, w_branch_a, w_branch_b, w_out, ffn2_norm, ffn2_w_gate, ffn2_w_up, ffn2_w_down, final_norm):
    b, t, d = x.shape
    assert ffn1_norm.shape[0] == 1, "single-layer trunk: the final norm is fused into the layer's last kernel"
    rows = t // GRID_W
    cos_t, sin_t = _rope_tables(t)
    bf = lambda w: w[0].astype(_BF16)
    row = lambda g: g.reshape(1, d).astype(_F32)
    seq3 = lambda a: a.reshape(b, t, a.shape[-1])

    h = _ffn1(x.reshape(b * t, d), row(ffn1_norm), bf(ffn1_w_gate), bf(ffn1_w_up), bf(ffn1_w_down))
    qat, ka, vat, qbt, kb, vbt, ga, gb = _inproj(h, row(mix_norm), bf(w_in), cos_t, sin_t, t)
    ya, yb = _mixers(qat, seq3(ka), vat, *_na_bias_tables(na_rpb[0].astype(_F32) * LOG2E, rows),
                     qbt, seq3(kb), vbt, sink_logit[0].astype(_F32))
    y = _merge(h, ya.reshape(b * t, -1), yb.reshape(b * t, -1), ga, gb,
               bf(w_branch_a), bf(w_branch_b), bf(w_out),
               row(ffn2_norm), bf(ffn2_w_gate), bf(ffn2_w_up), bf(ffn2_w_down), row(final_norm))
    return y.reshape(b, t, d)
```
